```python
import math
import jax, jax.numpy as jnp
from jax import lax
import numpy as np

D_MODEL = 4096
BATCH = 1
SEQ = 8192
DEPTH = 1
DEC_BATCH = 8
DEC_SEQ = 2048
PAST_LEN = 128

HEAD_DIM = 128
DN_HEADS = D_MODEL // (2 * HEAD_DIM)
DN_DIM = DN_HEADS * HEAD_DIM
AT_HEADS = D_MODEL // (2 * HEAD_DIM)
AT_KV_HEADS = max(AT_HEADS // 4, 1)
AT_GROUP = AT_HEADS // AT_KV_HEADS
AT_Q_DIM = AT_HEADS * HEAD_DIM
AT_KV_DIM = AT_KV_HEADS * HEAD_DIM
MIX_DIM = DN_DIM + AT_Q_DIM
CONV_K = 5
CHUNK = 64
Q_BLOCK = 128
GRID_W = 64
AXIS_DIM = HEAD_DIM // 2
ROPE_THETA = 10000.0
N_META = 16
PEER_HEADS = 8
N_KEYS = 128
N_EXPERTS = N_KEYS * N_KEYS
PEER_KEY_DIM = 256
PEER_HALF = PEER_KEY_DIM // 2
PEER_TOPK = 16
PEER_BLOCK = 64
EPS = 1e-6
IN_SPLITS = (DN_DIM, DN_DIM, DN_DIM, 2 * DN_HEADS, 2 * DN_HEADS, DN_DIM, AT_Q_DIM, AT_KV_DIM, AT_KV_DIM)
IN_DIM = sum(IN_SPLITS)

kernel_name = 'hybrid_deltanet_gqa_peer_encoder'


def _rms_f32(x, w):
    xf = x.astype(jnp.float32)
    return xf * lax.rsqrt(jnp.mean(xf * xf, -1, keepdims=True) + EPS) * w.astype(jnp.float32)


def _rmsnorm(x, w):
    return _rms_f32(x, w).astype(x.dtype)


def _l2norm(x):
    return x * lax.rsqrt(jnp.sum(x * x, -1, keepdims=True) + EPS)


def _dwconv(x, w):
    return lax.conv_general_dilated(
        x, w[:, None, :].astype(x.dtype), window_strides=(1,),
        padding=[(CONV_K // 2, CONV_K // 2)],
        dimension_numbers=('NWC', 'WIO', 'NWC'),
        feature_group_count=x.shape[-1])


def _axial_rope(n_tokens):
    rows = n_tokens // GRID_W
    row = jnp.repeat(jnp.arange(rows, dtype=jnp.float32), GRID_W)
    col = jnp.tile(jnp.arange(GRID_W, dtype=jnp.float32), rows)
    inv_freq = ROPE_THETA ** (-jnp.arange(0, AXIS_DIM, 2, dtype=jnp.float32) / AXIS_DIM)
    ar = row[:, None] * inv_freq
    ac = col[:, None] * inv_freq
    ang = jnp.concatenate([ar, ar, ac, ac], -1)
    ang = jnp.concatenate([jnp.zeros((N_META, HEAD_DIM), jnp.float32), ang], 0)
    return jnp.cos(ang), jnp.sin(ang)


def _rotate_half_axial(x):
    xs = x.reshape(*x.shape[:-1], 2, 2, HEAD_DIM // 4)
    return jnp.concatenate([-xs[..., 1:, :], xs[..., :1, :]], axis=-2).reshape(x.shape)


def _gated_delta_rule(q, k, v, g, beta):
    B, T, H, DK = q.shape
    DV = v.shape[-1]
    N = T // CHUNK

    def chunked(a):
        return jnp.moveaxis(a.reshape(B, N, CHUNK, H, *a.shape[3:]), 3, 2)

    qc = chunked(q) * (DK ** -0.5)
    kc, vc, gc, bc = chunked(k), chunked(v), chunked(g), chunked(beta)
    gcum = jnp.cumsum(gc, -1)
    tril = jnp.tril(jnp.ones((CHUNK, CHUNK), bool))
    strict = jnp.tril(jnp.ones((CHUNK, CHUNK), bool), -1)
    decay = jnp.exp(jnp.where(tril, gcum[..., :, None] - gcum[..., None, :], -jnp.inf))
    kk = jnp.einsum('bnhid,bnhjd->bnhij', kc, kc)
    lower = jnp.where(strict, bc[..., :, None] * kk * decay, 0.0) + jnp.eye(CHUNK, dtype=q.dtype)
    rhs = jnp.concatenate([vc * bc[..., None], kc * (bc * jnp.exp(gcum))[..., None]], -1)
    sol = lax.linalg.triangular_solve(lower, rhs, left_side=True, lower=True, unit_diagonal=True)
    u_c, w_c = sol[..., :DV], sol[..., DV:]
    qk = jnp.einsum('bnhid,bnhjd->bnhij', qc, kc) * decay
    q_dec = qc * jnp.exp(gcum)[..., None]
    k_dec = kc * jnp.exp(gcum[..., -1:] - gcum)[..., None]
    chunk_decay = jnp.exp(gcum[..., -1])

    def step(state, xs):
        q_i, k_i, u_i, w_i, qk_i, cd_i = xs
        v_new = u_i - jnp.einsum('bhck,bhkv->bhcv', w_i, state)
        o_i = jnp.einsum('bhck,bhkv->bhcv', q_i, state) + jnp.einsum('bhij,bhjv->bhiv', qk_i, v_new)
        state = state * cd_i[..., None, None] + jnp.einsum('bhck,bhcv->bhkv', k_i, v_new)
        return state, o_i

    xs = tuple(jnp.moveaxis(a, 1, 0) for a in (q_dec, k_dec, u_c, w_c, qk, chunk_decay))
    s0 = jnp.zeros((B, H, DK, DV), q.dtype)
    _, o = lax.scan(step, s0, xs)
    return jnp.moveaxis(o, 0, 1).swapaxes(2, 3).reshape(B, T, H, DV)


def _deltanet(dq, dk, dv, d_beta, d_alpha, d_gate, conv_w, a_log, dt_bias, out_norm):
    B, L, _ = dq.shape
    f32 = jnp.float32
    qkv = jax.nn.silu(_dwconv(jnp.concatenate([dq, dk, dv], -1), conv_w)).astype(f32)
    q, k, v = jnp.split(qkv, 3, axis=-1)
    q = _l2norm(q.reshape(B, L, DN_HEADS, HEAD_DIM))
    k = _l2norm(k.reshape(B, L, DN_HEADS, HEAD_DIM))
    v = v.reshape(B, L, DN_HEADS, HEAD_DIM)
    beta = jax.nn.sigmoid(d_beta.astype(f32)).reshape(B, L, 2, DN_HEADS)
    g = -jnp.exp(a_log.astype(f32)) * jax.nn.softplus(
        d_alpha.astype(f32).reshape(B, L, 2, DN_HEADS) + dt_bias.astype(f32))
    pad = (-N_META) % CHUNK

    def padf(a):
        return jnp.pad(a, [(0, 0), (pad, 0)] + [(0, 0)] * (a.ndim - 2))

    def rev(a):
        return jnp.flip(a, 1)

    qp, kp, vp = padf(q), padf(k), padf(v)
    o_fwd = _gated_delta_rule(qp, kp, vp, padf(g[:, :, 0]), padf(beta[:, :, 0]))
    o_bwd = rev(_gated_delta_rule(rev(qp), rev(kp), rev(vp), rev(padf(g[:, :, 1])), rev(padf(beta[:, :, 1]))))
    o = (o_fwd + o_bwd)[:, pad:]
    o = o * lax.rsqrt(jnp.mean(o * o, -1, keepdims=True) + EPS) * out_norm.astype(f32)
    o = o * jax.nn.silu(d_gate.astype(f32)).reshape(B, L, DN_HEADS, HEAD_DIM)
    return o.reshape(B, L, DN_DIM).astype(dq.dtype)


def _attention(aq, ak, av, q_norm, k_norm, out_norm, cos, sin):
    B, L, _ = aq.shape
    q = _rms_f32(aq.reshape(B, L, AT_KV_HEADS, AT_GROUP, HEAD_DIM), q_norm)
    k = _rms_f32(ak.reshape(B, L, AT_KV_HEADS, HEAD_DIM), k_norm)
    v = av.reshape(B, L, AT_KV_HEADS, HEAD_DIM).astype(jnp.float32)
    q = (q * cos[:, None, None] + _rotate_half_axial(q) * sin[:, None, None]) * (HEAD_DIM ** -0.5)
    k = k * cos[:, None] + _rotate_half_axial(k) * sin[:, None]

    def attend(qb):
        s = jnp.einsum('bqkgd,bskd->bkgqs', qb, k)
        p = jax.nn.softmax(s, axis=-1)
        return jnp.einsum('bkgqs,bskd->bqkgd', p, v)

    n_real = L - N_META
    o_meta = attend(q[:, :N_META])
    q_blocks = jnp.moveaxis(
        q[:, N_META:].reshape(B, n_real // Q_BLOCK, Q_BLOCK, AT_KV_HEADS, AT_GROUP, HEAD_DIM), 1, 0)
    o_real = jnp.moveaxis(lax.map(attend, q_blocks), 0, 1).reshape(
        B, n_real, AT_KV_HEADS, AT_GROUP, HEAD_DIM)
    o = jnp.concatenate([o_meta, o_real], 1).reshape(B, L, AT_Q_DIM)
    return _rms_f32(o, out_norm).astype(aq.dtype)


def _peer(h, wq, sub_keys, u, v):
    B, L, D = h.shape
    M = B * L
    pad = (-M) % PEER_BLOCK
    ht = jnp.pad(h.reshape(M, D), ((0, pad), (0, 0)))
    keys = sub_keys.astype(jnp.float32)

    def block(hb):
        q = (hb @ wq).astype(jnp.float32).reshape(PEER_BLOCK, PEER_HEADS, 2, PEER_HALF)
        s = jnp.einsum('thpd,hpnd->thpn', q, keys)
        sv, si = lax.top_k(s, PEER_TOPK)
        cand = (sv[:, :, 0, :, None] + sv[:, :, 1, None, :]).reshape(PEER_BLOCK, PEER_HEADS, PEER_TOPK * PEER_TOPK)
        cidx = (si[:, :, 0, :, None] * N_KEYS + si[:, :, 1, None, :]).reshape(PEER_BLOCK, PEER_HEADS, PEER_TOPK * PEER_TOPK)
        top_s, pos = lax.top_k(cand, PEER_TOPK)
        eidx = jnp.take_along_axis(cidx, pos, -1).reshape(PEER_BLOCK, PEER_HEADS * PEER_TOPK)
        gate = jax.nn.softmax(top_s, -1).reshape(PEER_BLOCK, PEER_HEADS * PEER_TOPK)
        ue = jnp.take(u, eidx, axis=0)
        act = jax.nn.gelu(jnp.einsum('ted,td->te', ue, hb).astype(jnp.float32), approximate=False) * gate
        out = jnp.einsum('te,ted->td', act.astype(v.dtype), jnp.take(v, eidx, axis=0))
        return out.astype(hb.dtype)

    out = lax.map(block, ht.reshape(-1, PEER_BLOCK, D))
    return out.reshape(-1, D)[:M].reshape(B, L, D)


def _layer(h, cos, sin, norm1, w_in, conv_w, a_log, dt_bias, dn_out_norm, q_norm, k_norm,
           attn_out_norm, w_out, norm2, peer_wq, peer_keys, peer_u, peer_v):
    proj = _rmsnorm(h, norm1) @ w_in
    offs = [int(o) for o in np.cumsum(IN_SPLITS)[:-1]]
    dq, dk, dv, d_beta, d_alpha, d_gate, aq, ak, av = jnp.split(proj, offs, axis=-1)
    o_dn = _deltanet(dq, dk, dv, d_beta, d_alpha, d_gate, conv_w, a_log, dt_bias, dn_out_norm)
    o_at = _attention(aq, ak, av, q_norm, k_norm, attn_out_norm, cos, sin)
    h = h + jnp.concatenate([o_dn, o_at], -1) @ w_out
    h = h + _peer(_rmsnorm(h, norm2), peer_wq, peer_keys, peer_u, peer_v)
    return h


def _encode(x, meta_tokens, norm1, w_in, conv_w, a_log, dt_bias, dn_out_norm, q_norm, k_norm,
            attn_out_norm, w_out, norm2, peer_wq, peer_keys, peer_u, peer_v):
    B, S, D = x.shape
    cos, sin = _axial_rope(S)
    h = jnp.concatenate([jnp.broadcast_to(meta_tokens.astype(x.dtype)[None], (B, N_META, D)), x], axis=1)
    for layer in range(DEPTH):
        h = _layer(h, cos, sin, norm1[layer], w_in[layer], conv_w[layer], a_log[layer], dt_bias[layer],
                   dn_out_norm[layer], q_norm[layer], k_norm[layer], attn_out_norm[layer], w_out[layer],
                   norm2[layer], peer_wq[layer], peer_keys[layer], peer_u[layer], peer_v[layer])
    return h[:, N_META:]


def setup_inputs(seed: int = 0) -> dict:
    key = jax.random.key(seed)
    ks = jax.random.split(key, 20)
    f32 = jnp.float32

    def nrm(k, shape, scale):
        return jax.random.normal(k, shape, f32) * scale

    dt = jnp.exp(jax.random.uniform(ks[6], (DEPTH, 2, DN_HEADS), f32, math.log(1e-3), math.log(1e-1)))
    return {
        'x_prompt': nrm(ks[0], (BATCH, SEQ, D_MODEL), 1.0),
        'x_sample': nrm(ks[1], (DEC_BATCH, DEC_SEQ, D_MODEL), 1.0),
        'meta_tokens': nrm(ks[2], (N_META, D_MODEL), 1.0),
        'norm1': 1.0 + nrm(ks[3], (DEPTH, D_MODEL), 0.1),
        'w_in': nrm(ks[4], (DEPTH, D_MODEL, IN_DIM), D_MODEL ** -0.5),
        'conv_w': nrm(ks[5], (DEPTH, CONV_K, 3 * DN_DIM), CONV_K ** -0.5),
        'a_log': jnp.log(jax.random.uniform(ks[7], (DEPTH, 2, DN_HEADS), f32, 1.0, 16.0)),
        'dt_bias': dt + jnp.log(-jnp.expm1(-dt)),
        'dn_out_norm': 1.0 + nrm(ks[8], (DEPTH, HEAD_DIM), 0.1),
        'q_norm': 1.0 + nrm(ks[9], (DEPTH, HEAD_DIM), 0.1),
        'k_norm': 1.0 + nrm(ks[10], (DEPTH, HEAD_DIM), 0.1),
        'attn_out_norm': 1.0 + nrm(ks[11], (DEPTH, AT_Q_DIM), 0.1),
        'w_out': nrm(ks[12], (DEPTH, MIX_DIM, D_MODEL), MIX_DIM ** -0.5),
        'norm2': 1.0 + nrm(ks[13], (DEPTH, D_MODEL), 0.1),
        'peer_wq': nrm(ks[14], (DEPTH, D_MODEL, PEER_HEADS * PEER_KEY_DIM), D_MODEL ** -0.5),
        'peer_keys': nrm(ks[15], (DEPTH, PEER_HEADS, 2, N_KEYS, PEER_HALF), PEER_HALF ** -0.5),
        'peer_u': nrm(ks[16], (DEPTH, N_EXPERTS, D_MODEL), D_MODEL ** -0.5),
        'peer_v': nrm(ks[17], (DEPTH, N_EXPERTS, D_MODEL), 0.5),
    }


def reference(x_prompt, x_sample, meta_tokens, norm1, w_in, conv_w, a_log, dt_bias, dn_out_norm,
              q_norm, k_norm, attn_out_norm, w_out, norm2, peer_wq, peer_keys, peer_u, peer_v):
    y_prompt = _encode(x_prompt, meta_tokens, norm1, w_in, conv_w, a_log, dt_bias, dn_out_norm, q_norm,
                       k_norm, attn_out_norm, w_out, norm2, peer_wq, peer_keys, peer_u, peer_v)
    y_sample = _encode(x_sample, meta_tokens, norm1, w_in, conv_w, a_log, dt_bias, dn_out_norm, q_norm,
                       k_norm, attn_out_norm, w_out, norm2, peer_wq, peer_keys, peer_u, peer_v)
    return (y_prompt, y_sample)
```

```python
import functools

import jax
import jax.numpy as jnp
from jax import lax
from jax.experimental import pallas as pl
from jax.experimental.pallas import tpu as pltpu

F32 = jnp.float32
BF16 = jnp.bfloat16

HEAD_DIM = 128
DN_HEADS = 16
DN_DIM = DN_HEADS * HEAD_DIM
AT_HEADS = 16
AT_KV_HEADS = 4
AT_GROUP = AT_HEADS // AT_KV_HEADS
AT_Q_DIM = AT_HEADS * HEAD_DIM
AT_KV_DIM = AT_KV_HEADS * HEAD_DIM
CONV_K = 5
CHUNK = 64
GRID_W = 64
AXIS_DIM = HEAD_DIM // 2
ROPE_THETA = 10000.0
N_META = 16
PEER_HEADS = 8
N_KEYS = 128
PEER_KEY_DIM = 256
PEER_TOPK = 16
EPS = 1e-6
IN_SPLITS = (DN_DIM, DN_DIM, DN_DIM, 2 * DN_HEADS, 2 * DN_HEADS, DN_DIM, AT_Q_DIM, AT_KV_DIM, AT_KV_DIM)

COL_DQ, COL_DK, COL_DV, COL_GATE, COL_AQ = 0, DN_DIM, 2 * DN_DIM, 3 * DN_DIM, 4 * DN_DIM
COL_AK = COL_AQ + AT_Q_DIM
COL_AV = COL_AK + AT_KV_DIM
BIG_DIM = COL_AV + AT_KV_DIM

LANES = 128
META_PAD = CHUNK - N_META
VMEM_LIMIT = 56 * 1024 * 1024

NT_DIMS = (((1,), (1,)), ((), ()))
TN_DIMS = (((0,), (0,)), ((), ()))


def _cparams(*sem):
    return pltpu.CompilerParams(dimension_semantics=sem, vmem_limit_bytes=VMEM_LIMIT)


def _rmsnorm_kernel(x_ref, w_ref, o_ref):
    x = x_ref[...]
    ms = jnp.mean(x * x, axis=-1, keepdims=True)
    o_ref[...] = (x * lax.rsqrt(ms + EPS) * w_ref[...]).astype(o_ref.dtype)


def _rmsnorm(x, w, tm):
    m, d = x.shape
    return pl.pallas_call(
        _rmsnorm_kernel,
        grid=(m // tm,),
        in_specs=[pl.BlockSpec((tm, d), lambda i: (i, 0)), pl.BlockSpec((1, d), lambda i: (0, 0))],
        out_specs=pl.BlockSpec((tm, d), lambda i: (i, 0)),
        out_shape=jax.ShapeDtypeStruct((m, d), BF16),
        compiler_params=_cparams("parallel"),
        name="rmsnorm",
    )(x, w.reshape(1, d))


def _mm_kernel(x_ref, w_ref, o_ref):
    o_ref[...] = jnp.dot(x_ref[...], w_ref[...], preferred_element_type=F32).astype(o_ref.dtype)


def _matmul(x, w, tm, tn, out_dtype=F32):
    m, k = x.shape
    n = w.shape[1]
    return pl.pallas_call(
        _mm_kernel,
        grid=(m // tm, n // tn),
        in_specs=[pl.BlockSpec((tm, k), lambda i, j: (i, 0)), pl.BlockSpec((k, tn), lambda i, j: (0, j))],
        out_specs=pl.BlockSpec((tm, tn), lambda i, j: (i, j)),
        out_shape=jax.ShapeDtypeStruct((m, n), out_dtype),
        compiler_params=_cparams("parallel", "arbitrary"),
        name="matmul",
    )(x, w)


def _outproj_kernel(a_ref, b_ref, wa_ref, wb_ref, x_ref, o_ref):
    acc = jnp.dot(a_ref[...], wa_ref[...], preferred_element_type=F32)
    acc += jnp.dot(b_ref[...], wb_ref[...], preferred_element_type=F32)
    o_ref[...] = x_ref[...] + acc


def _outproj(o_dn, o_at, w_out, x, tm, tn):
    m, k = o_dn.shape
    n = w_out.shape[1]
    return pl.pallas_call(
        _outproj_kernel,
        grid=(m // tm, n // tn),
        in_specs=[
            pl.BlockSpec((tm, k), lambda i, j: (i, 0)),
            pl.BlockSpec((tm, k), lambda i, j: (i, 0)),
            pl.BlockSpec((k, tn), lambda i, j: (0, j)),
            pl.BlockSpec((k, tn), lambda i, j: (1, j)),
            pl.BlockSpec((tm, tn), lambda i, j: (i, j)),
        ],
        out_specs=pl.BlockSpec((tm, tn), lambda i, j: (i, j)),
        out_shape=jax.ShapeDtypeStruct((m, n), F32),
        compiler_params=_cparams("parallel", "arbitrary"),
        name="outproj",
    )(o_dn, o_at, w_out, w_out, x)


CONV_LEAD = 8 + META_PAD + N_META
CONV_ROWS = 512


def _dn_prep_kernel(xq, xk, xv, mq, mk, mv, wq, wk, wv, oq, ok, ov, buf, *, seq):
    half = CONV_K // 2

    def run(x_ref, m_ref, w_ref, o_ref, normalise, scale):
        buf[0:CONV_LEAD - N_META, :] = jnp.zeros((CONV_LEAD - N_META, LANES), F32)
        buf[CONV_LEAD - N_META:CONV_LEAD, :] = m_ref[...]
        for c in range(seq // CONV_ROWS):
            buf[CONV_LEAD + c * CONV_ROWS:CONV_LEAD + (c + 1) * CONV_ROWS, :] = x_ref[c * CONV_ROWS:(c + 1) * CONV_ROWS, :]
        buf[CONV_LEAD + seq:CONV_LEAD + seq + 8, :] = jnp.zeros((8, LANES), F32)
        w = w_ref[...]

        def act(t0, n):
            y = w[0:1, :] * buf[t0 + 8 - half:t0 + 8 - half + n, :]
            for k in range(1, CONV_K):
                y = y + w[k:k + 1, :] * buf[t0 + 8 - half + k:t0 + 8 - half + k + n, :]
            y = y * jax.nn.sigmoid(y)
            if normalise:
                y = y * lax.rsqrt(jnp.sum(y * y, axis=-1, keepdims=True) + EPS) * scale
            return y.astype(o_ref.dtype)

        o_ref[0:META_PAD, :] = jnp.zeros((META_PAD, LANES), o_ref.dtype)
        o_ref[META_PAD:CHUNK, :] = act(META_PAD, N_META)
        for c in range(seq // CONV_ROWS):
            o_ref[CHUNK + c * CONV_ROWS:CHUNK + (c + 1) * CONV_ROWS, :] = act(CHUNK + c * CONV_ROWS, CONV_ROWS)

    run(xq, mq, wq, oq, True, HEAD_DIM ** -0.5)
    run(xk, mk, wk, ok, True, 1.0)
    run(xv, mv, wv, ov, False, 1.0)


def _dn_prep(proj, proj_meta, conv_w, batch, seq):
    t = seq + CHUNK
    proj3 = proj.reshape(batch, seq, BIG_DIM)
    hq, hk, hv = COL_DQ // LANES, COL_DK // LANES, COL_DV // LANES

    def xspec(off):
        return pl.BlockSpec((None, seq, LANES), lambda b, h: (b, 0, off + h))

    def mspec(off):
        return pl.BlockSpec((N_META, LANES), lambda b, h: (0, off + h))

    def wspec(off):
        return pl.BlockSpec((CONV_K, LANES), lambda b, h: (0, off + h))

    ospec = pl.BlockSpec((None, t, LANES), lambda b, h: (b, 0, h))
    oshape = jax.ShapeDtypeStruct((batch, t, DN_DIM), BF16)
    return pl.pallas_call(
        functools.partial(_dn_prep_kernel, seq=seq),
        grid=(batch, DN_HEADS),
        in_specs=[xspec(hq), xspec(hk), xspec(hv), mspec(hq), mspec(hk), mspec(hv), wspec(hq), wspec(hk), wspec(hv)],
        out_specs=[ospec, ospec, ospec],
        out_shape=[oshape, oshape, oshape],
        scratch_shapes=[pltpu.VMEM((seq + CONV_LEAD + 8, LANES), F32)],
        compiler_params=_cparams("parallel", "parallel"),
        name="dn_prep",
    )(proj3, proj3, proj3, proj_meta, proj_meta, proj_meta, conv_w, conv_w, conv_w)


def _dn_scan_kernel(q_ref, k_ref, v_ref, abc_ref, abr_ref, pc_ref, pr_ref, o_ref, state):
    d = pl.program_id(1)
    n = pl.program_id(2)
    n_chunks = pl.num_programs(2)
    fwd = d == 0
    n_eff = jnp.where(fwd, n, n_chunks - 1 - n)

    @pl.when(n == 0)
    def _():
        state[...] = jnp.zeros_like(state)

    row = lax.broadcasted_iota(jnp.int32, (CHUNK, CHUNK), 0)
    col = lax.broadcasted_iota(jnp.int32, (CHUNK, CHUNK), 1)
    order = (row - col) * jnp.where(fwd, 1, -1)
    incl = order >= 0
    strict = order > 0
    eye = (row == col).astype(F32)
    levels = []
    for sh in range(6):
        levels.append(jnp.logical_and((row >> (sh + 1)) == (col >> (sh + 1)), (row >> sh) != (col >> sh)))

    first = n_eff == 0
    t_c = lax.broadcasted_iota(jnp.int32, (CHUNK, 1), 0)
    t_r = lax.broadcasted_iota(jnp.int32, (1, CHUNK), 1)
    live_c = jnp.logical_or(jnp.logical_not(first), t_c >= META_PAD)
    live_r = jnp.logical_or(jnp.logical_not(first), t_r >= META_PAD)

    abc = abc_ref[...]
    abr = abr_ref[...]
    beta_c = jnp.where(live_c, jax.nn.sigmoid(abc[:, 0:DN_HEADS]), 0.0)
    g_c = jnp.where(live_c, -jnp.exp(pc_ref[0:1, :]) * jax.nn.softplus(abc[:, DN_HEADS:] + pc_ref[1:2, :]), 0.0)
    g_r = jnp.where(live_r, -jnp.exp(pr_ref[:, 0:1]) * jax.nn.softplus(abr[DN_HEADS:, :] + pr_ref[:, 1:2]), 0.0)

    tri = incl.astype(F32)
    hi = lax.Precision.HIGHEST
    gcum_c = jnp.dot(tri, g_c, precision=hi, preferred_element_type=F32)
    gcum_r = lax.dot_general(g_r, tri, NT_DIMS, precision=hi, preferred_element_type=F32)
    gtot = jnp.sum(g_c, axis=0, keepdims=True)
    eg = jnp.exp(gcum_c)
    ekd = jnp.exp(gtot - gcum_c)
    cdec = jnp.exp(gtot)

    for h in range(DN_HEADS):
        sl = slice(h * HEAD_DIM, (h + 1) * HEAD_DIM)
        qh = q_ref[:, sl]
        kh = k_ref[:, sl]
        vh = v_ref[:, sl]
        gc = gcum_c[:, h:h + 1]
        gr = gcum_r[h:h + 1, :]
        bc = beta_c[:, h:h + 1]
        egh = eg[:, h:h + 1]
        decay = jnp.exp(jnp.where(incl, gc - gr, -jnp.inf))
        kk = lax.dot_general(kh, kh, NT_DIMS, preferred_element_type=F32)
        a = jnp.where(strict, bc * kk * decay, 0.0)
        tinv = eye - jnp.where(levels[0], a, 0.0)
        for lvl in levels[1:]:
            cb = jnp.where(lvl, a, 0.0).astype(BF16)
            tb = tinv.astype(BF16)
            tc = jnp.dot(tb, cb, preferred_element_type=F32).astype(BF16)
            tinv = tinv - jnp.dot(tc, tb, preferred_element_type=F32)
        kf = kh.astype(F32)
        rhs = jnp.concatenate([vh.astype(F32) * bc, kf * (bc * egh)], axis=1).astype(BF16)
        sol = jnp.dot(tinv.astype(BF16), rhs, preferred_element_type=F32)
        u = sol[:, :HEAD_DIM]
        w = sol[:, HEAD_DIM:]
        qk = lax.dot_general(qh, kh, NT_DIMS, preferred_element_type=F32) * decay
        q_dec = (qh.astype(F32) * egh).astype(BF16)
        k_dec = (kf * ekd[:, h:h + 1]).astype(BF16)
        s = state[h]
        sb = s.astype(BF16)
        v_new = u - jnp.dot(w.astype(BF16), sb, preferred_element_type=F32)
        vb = v_new.astype(BF16)
        o = jnp.dot(q_dec, sb, preferred_element_type=F32) + jnp.dot(qk.astype(BF16), vb, preferred_element_type=F32)
        state[h] = s * cdec[:, h:h + 1] + lax.dot_general(k_dec, vb, TN_DIMS, preferred_element_type=F32)

        @pl.when(n_eff > 0)
        def _():
            o_ref[:, sl] = o


def _dn_scan(q, k, v, ab_col, ab_row, par_col, par_row, batch, seq):
    n_chunks = (seq + CHUNK) // CHUNK

    def chunk(d, n):
        return jnp.where(d == 0, n, n_chunks - 1 - n)

    qspec = pl.BlockSpec((None, CHUNK, DN_DIM), lambda b, d, n: (b, chunk(d, n), 0))
    return pl.pallas_call(
        _dn_scan_kernel,
        grid=(batch, 2, n_chunks),
        in_specs=[
            qspec, qspec, qspec,
            pl.BlockSpec((None, None, CHUNK, 2 * DN_HEADS), lambda b, d, n: (b, d, chunk(d, n), 0)),
            pl.BlockSpec((None, None, None, 2 * DN_HEADS, CHUNK), lambda b, d, n: (b, d, chunk(d, n), 0, 0)),
            pl.BlockSpec((None, 2, DN_HEADS), lambda b, d, n: (d, 0, 0)),
            pl.BlockSpec((None, DN_HEADS, 2), lambda b, d, n: (d, 0, 0)),
        ],
        out_specs=pl.BlockSpec((None, None, CHUNK, DN_DIM), lambda b, d, n: (b, d, jnp.maximum(chunk(d, n) - 1, 0), 0)),
        out_shape=jax.ShapeDtypeStruct((batch, 2, seq, DN_DIM), F32),
        scratch_shapes=[pltpu.VMEM((DN_HEADS, HEAD_DIM, HEAD_DIM), F32)],
        compiler_params=_cparams("parallel", "arbitrary", "arbitrary"),
        name="dn_scan",
    )(q, k, v, ab_col, ab_row, par_col, par_row)


def _dn_post_kernel(o_ref, g_ref, w_ref, y_ref):
    w = w_ref[...]
    for h in range(DN_HEADS):
        sl = slice(h * HEAD_DIM, (h + 1) * HEAD_DIM)
        o = o_ref[0, :, sl] + o_ref[1, :, sl]
        o = o * lax.rsqrt(jnp.mean(o * o, axis=-1, keepdims=True) + EPS) * w
        g = g_ref[:, sl]
        y_ref[:, sl] = (o * (g * jax.nn.sigmoid(g))).astype(y_ref.dtype)


def _dn_post(o, proj, out_norm, batch, seq, tm):
    nb = seq // tm
    return pl.pallas_call(
        _dn_post_kernel,
        grid=(batch, nb),
        in_specs=[
            pl.BlockSpec((None, 2, tm, DN_DIM), lambda b, i: (b, 0, i, 0)),
            pl.BlockSpec((tm, DN_DIM), lambda b, i: (b * nb + i, COL_GATE // DN_DIM)),
            pl.BlockSpec((1, HEAD_DIM), lambda b, i: (0, 0)),
        ],
        out_specs=pl.BlockSpec((tm, DN_DIM), lambda b, i: (b * nb + i, 0)),
        out_shape=jax.ShapeDtypeStruct((batch * seq, DN_DIM), BF16),
        compiler_params=_cparams("parallel", "parallel"),
        name="dn_post",
    )(o, proj, out_norm.reshape(1, HEAD_DIM))


def _rope_tables(seq):
    rows = seq // GRID_W
    row = jnp.repeat(jnp.arange(rows, dtype=F32), GRID_W)
    col = jnp.tile(jnp.arange(GRID_W, dtype=F32), rows)
    inv_freq = ROPE_THETA ** (-jnp.arange(0, AXIS_DIM, 2, dtype=F32) / AXIS_DIM)
    ar = row[:, None] * inv_freq
    ac = col[:, None] * inv_freq
    ang = jnp.concatenate([ar, ar, ac, ac], -1)
    return jnp.cos(ang), jnp.sin(ang)


def _attn_prep_kernel(q_ref, k_ref, v_ref, cos_ref, sin_ref, qw_ref, kw_ref, qo_ref, ko_ref, vo_ref):
    cos = cos_ref[...]
    sin = sin_ref[...]
    lane = lax.broadcasted_iota(jnp.int32, cos.shape, 1)
    first_half = (lane % (HEAD_DIM // 2)) < (HEAD_DIM // 4)

    def norm_rope(x, w, scale):
        x = x * lax.rsqrt(jnp.mean(x * x, axis=-1, keepdims=True) + EPS) * w
        rot = jnp.where(first_half, -pltpu.roll(x, HEAD_DIM - HEAD_DIM // 4, 1), pltpu.roll(x, HEAD_DIM // 4, 1))
        return (x * cos + rot * sin) * scale

    for h in range(AT_HEADS):
        sl = slice(h * HEAD_DIM, (h + 1) * HEAD_DIM)
        qo_ref[:, sl] = norm_rope(q_ref[:, sl], qw_ref[...], HEAD_DIM ** -0.5).astype(qo_ref.dtype)
    for h in range(AT_KV_HEADS):
        sl = slice(h * HEAD_DIM, (h + 1) * HEAD_DIM)
        ko_ref[:, sl] = norm_rope(k_ref[:, sl], kw_ref[...], 1.0).astype(ko_ref.dtype)
    vo_ref[...] = v_ref[...].astype(vo_ref.dtype)


def _attn_prep(proj, cos, sin, q_norm, k_norm, tm):
    m = proj.shape[0]
    nrep = cos.shape[0] // tm
    return pl.pallas_call(
        _attn_prep_kernel,
        grid=(m // tm,),
        in_specs=[
            pl.BlockSpec((tm, AT_Q_DIM), lambda i: (i, COL_AQ // AT_Q_DIM)),
            pl.BlockSpec((tm, AT_KV_DIM), lambda i: (i, COL_AK // AT_KV_DIM)),
            pl.BlockSpec((tm, AT_KV_DIM), lambda i: (i, COL_AV // AT_KV_DIM)),
            pl.BlockSpec((tm, HEAD_DIM), lambda i: (i % nrep, 0)),
            pl.BlockSpec((tm, HEAD_DIM), lambda i: (i % nrep, 0)),
            pl.BlockSpec((1, HEAD_DIM), lambda i: (0, 0)),
            pl.BlockSpec((1, HEAD_DIM), lambda i: (0, 0)),
        ],
        out_specs=[
            pl.BlockSpec((tm, AT_Q_DIM), lambda i: (i, 0)),
            pl.BlockSpec((tm, AT_KV_DIM), lambda i: (i, 0)),
            pl.BlockSpec((tm, AT_KV_DIM), lambda i: (i, 0)),
        ],
        out_shape=[
            jax.ShapeDtypeStruct((m, AT_Q_DIM), BF16),
            jax.ShapeDtypeStruct((m, AT_KV_DIM), BF16),
            jax.ShapeDtypeStruct((m, AT_KV_DIM), BF16),
        ],
        compiler_params=_cparams("parallel"),
        name="attn_prep",
    )(proj, proj, proj, cos, sin, q_norm.reshape(1, HEAD_DIM), k_norm.reshape(1, HEAD_DIM))


def _flash_kernel(q_ref, k_ref, v_ref, km_ref, vm_ref, w_ref, o_ref, o_scr, *, seq, tq, tk):
    lane = lax.broadcasted_iota(jnp.int32, (1, LANES), 1)
    is_meta = lane < N_META
    for g in range(AT_KV_HEADS):
        ks = slice(g * HEAD_DIM, (g + 1) * HEAD_DIM)
        qg = jnp.concatenate(
            [q_ref[:, (AT_GROUP * g + j) * HEAD_DIM:(AT_GROUP * g + j + 1) * HEAD_DIM] for j in range(AT_GROUP)], axis=0)
        s0 = lax.dot_general(qg, km_ref[:, ks], NT_DIMS, preferred_element_type=F32)
        s0 = jnp.where(is_meta, s0, -jnp.inf)
        m0 = jnp.max(s0, axis=-1, keepdims=True)
        p0 = jnp.exp(s0 - m0)
        l0 = jnp.sum(p0, axis=-1, keepdims=True)
        acc0 = jnp.dot(p0.astype(BF16), vm_ref[:, ks], preferred_element_type=F32)

        def body(j, carry):
            m, l, acc = carry
            start = pl.multiple_of(j * tk, tk)
            kb = k_ref[pl.ds(start, tk), ks]
            vb = v_ref[pl.ds(start, tk), ks]
            s = lax.dot_general(qg, kb, NT_DIMS, preferred_element_type=F32)
            m_new = jnp.maximum(m, jnp.max(s, axis=-1, keepdims=True))
            alpha = jnp.exp(m - m_new)
            p = jnp.exp(s - m_new)
            l = alpha * l + jnp.sum(p, axis=-1, keepdims=True)
            acc = alpha * acc + jnp.dot(p.astype(BF16), vb, preferred_element_type=F32)
            return m_new, l, acc

        _, l, acc = lax.fori_loop(0, seq // tk, body, (m0, l0, acc0))
        og = acc / l
        for j in range(AT_GROUP):
            h = AT_GROUP * g + j
            o_scr[:, h * HEAD_DIM:(h + 1) * HEAD_DIM] = og[j * tq:(j + 1) * tq, :]
    o = o_scr[...]
    o_ref[...] = (o * lax.rsqrt(jnp.mean(o * o, axis=-1, keepdims=True) + EPS) * w_ref[...]).astype(o_ref.dtype)


def _flash(q, k, v, k_meta, v_meta, out_norm, batch, seq, tq, tk):
    nq = seq // tq
    return pl.pallas_call(
        functools.partial(_flash_kernel, seq=seq, tq=tq, tk=tk),
        grid=(batch, nq),
        in_specs=[
            pl.BlockSpec((tq, AT_Q_DIM), lambda b, i: (b * nq + i, 0)),
            pl.BlockSpec((seq, AT_KV_DIM), lambda b, i: (b, 0)),
            pl.BlockSpec((seq, AT_KV_DIM), lambda b, i: (b, 0)),
            pl.BlockSpec((LANES, AT_KV_DIM), lambda b, i: (0, 0)),
            pl.BlockSpec((LANES, AT_KV_DIM), lambda b, i: (0, 0)),
            pl.BlockSpec((1, AT_Q_DIM), lambda b, i: (0, 0)),
        ],
        out_specs=pl.BlockSpec((tq, AT_Q_DIM), lambda b, i: (b * nq + i, 0)),
        out_shape=jax.ShapeDtypeStruct((batch * seq, AT_Q_DIM), BF16),
        scratch_shapes=[pltpu.VMEM((tq, AT_Q_DIM), F32)],
        compiler_params=_cparams("parallel", "arbitrary"),
        name="flash",
    )(q, k, v, k_meta, v_meta, out_norm.reshape(1, AT_Q_DIM))


def _top_desc(x, count):
    rows = x.shape[0]
    ridx = lax.broadcasted_iota(jnp.int32, x.shape, 0).astype(F32)
    out = []
    for _ in range(count):
        m = jnp.max(x, axis=0, keepdims=True)
        out.append(m)
        first = jnp.min(jnp.where(x == m, ridx, float(rows)), axis=0, keepdims=True)
        x = jnp.where(ridx == first, -jnp.inf, x)
    return out


def _peer_route_kernel(h_ref, wq_ref, keys_ref, s1_ref, s2_ref, a_ref, b_ref, thr_ref):
    q_t = lax.dot_general(wq_ref[...], h_ref[...], NT_DIMS, preferred_element_type=F32)
    half = PEER_KEY_DIM // 2
    s1 = jnp.dot(keys_ref[0], q_t[:half].astype(BF16), preferred_element_type=F32)
    s2 = jnp.dot(keys_ref[1], q_t[half:].astype(BF16), preferred_element_type=F32)
    t1 = _top_desc(s1, PEER_TOPK)
    t2 = _top_desc(s2, PEER_TOPK)
    t2cat = jnp.concatenate(t2, axis=0)
    cand = jnp.concatenate([t1[i] + t2cat for i in range(PEER_TOPK)], axis=0)
    top = _top_desc(cand, PEER_TOPK)
    z = jnp.exp(top[0] - top[0])
    for i in range(1, PEER_TOPK):
        z = z + jnp.exp(top[i] - top[0])
    s1_ref[...] = s1
    s2_ref[...] = s2
    a_ref[...] = jnp.exp(s1 - t1[0]) / z
    b_ref[...] = jnp.exp(s2 - t2[0])
    thr_ref[...] = top[PEER_TOPK - 1]


def _peer_route(hn, wq_t, keys, tm):
    m, d = hn.shape
    spec = pl.BlockSpec((None, N_KEYS, tm), lambda i, h: (h, 0, i))
    shape = jax.ShapeDtypeStruct((PEER_HEADS, N_KEYS, m), F32)
    return pl.pallas_call(
        _peer_route_kernel,
        grid=(m // tm, PEER_HEADS),
        in_specs=[
            pl.BlockSpec((tm, d), lambda i, h: (i, 0)),
            pl.BlockSpec((PEER_KEY_DIM, d), lambda i, h: (h, 0)),
            pl.BlockSpec((None, 2, N_KEYS, PEER_KEY_DIM // 2), lambda i, h: (h, 0, 0, 0)),
        ],
        out_specs=[spec, spec, spec, spec, pl.BlockSpec((None, 1, tm), lambda i, h: (h, 0, i))],
        out_shape=[shape, shape, shape, shape, jax.ShapeDtypeStruct((PEER_HEADS, 1, m), F32)],
        compiler_params=_cparams("parallel", "arbitrary"),
        name="peer_route",
    )(hn, wq_t, keys)


def _peer_dense_kernel(hn_ref, res_ref, u_ref, v_ref, s1_ref, s2_ref, a_ref, b_ref, thr_ref, o_ref, act_ref, *, te):
    j = pl.program_id(1)

    @pl.when(j == 0)
    def _():
        o_ref[...] = res_ref[...]

    s_t = lax.dot_general(u_ref[...], hn_ref[...], NT_DIMS, preferred_element_type=F32)
    per = te // N_KEYS
    for ii in range(per):
        i = j * per + ii
        gate = None
        for h in range(PEER_HEADS):
            score = s1_ref[h, pl.ds(i, 1), :] + s2_ref[h]
            term = jnp.where(score >= thr_ref[h], b_ref[h], 0.0) * a_ref[h, pl.ds(i, 1), :]
            gate = term if gate is None else gate + term
        x = s_t[ii * N_KEYS:(ii + 1) * N_KEYS, :]
        act = 0.5 * x * (1.0 + lax.erf(x * (2.0 ** -0.5))) * gate
        act_ref[ii * N_KEYS:(ii + 1) * N_KEYS, :] = act.astype(act_ref.dtype)
    o_ref[...] += lax.dot_general(act_ref[...], v_ref[...], TN_DIMS, preferred_element_type=F32)


def _peer_dense(hn, res, u, v, s1, s2, a, b, thr, tm, te):
    m, d = hn.shape
    n_exp = u.shape[0]
    once = pl.Buffered(1)
    tok = pl.BlockSpec((PEER_HEADS, N_KEYS, tm), lambda i, j: (0, 0, i), pipeline_mode=once)
    return pl.pallas_call(
        functools.partial(_peer_dense_kernel, te=te),
        grid=(m // tm, n_exp // te),
        in_specs=[
            pl.BlockSpec((tm, d), lambda i, j: (i, 0), pipeline_mode=once),
            pl.BlockSpec((tm, d), lambda i, j: (i, 0), pipeline_mode=once),
            pl.BlockSpec((te, d), lambda i, j: (j, 0)),
            pl.BlockSpec((te, d), lambda i, j: (j, 0)),
            tok, tok, tok, tok,
            pl.BlockSpec((PEER_HEADS, 1, tm), lambda i, j: (0, 0, i), pipeline_mode=once),
        ],
        out_specs=pl.BlockSpec((tm, d), lambda i, j: (i, 0)),
        out_shape=jax.ShapeDtypeStruct((m, d), F32),
        scratch_shapes=[pltpu.VMEM((te, tm), BF16)],
        compiler_params=_cparams("parallel", "arbitrary"),
        name="peer_dense",
    )(hn, res, u, v, s1, s2, a, b, thr)


def _tile(m, pref):
    return pref if m % pref == 0 else m


def _encode(x, meta, wts):
    batch, seq, d = x.shape
    m = batch * seq
    xf = x.reshape(m, d)

    xn = _rmsnorm(xf, wts["norm1"], _tile(m, 256))
    proj = _matmul(xn, wts["w_big"], _tile(m, 1024), 512)
    ab = _matmul(xn, wts["w_small"], _tile(m, 1024), LANES)
    proj_meta, ab_meta = meta["proj"], meta["ab"]

    q, k, v = _dn_prep(proj, proj_meta, wts["conv_w"], batch, seq)
    t = seq + CHUNK
    ab_seq = jnp.concatenate([
        jnp.zeros((batch, META_PAD, 4 * DN_HEADS), F32),
        jnp.broadcast_to(ab_meta[None, :, :4 * DN_HEADS], (batch, N_META, 4 * DN_HEADS)),
        ab[:, :4 * DN_HEADS].reshape(batch, seq, 4 * DN_HEADS)], axis=1)
    ab_seq = ab_seq.reshape(batch, t, 2, 2, DN_HEADS)
    ab_col = jnp.transpose(ab_seq, (0, 3, 1, 2, 4)).reshape(batch, 2, t, 2 * DN_HEADS)
    ab_row = jnp.swapaxes(ab_col.reshape(batch, 2, t // CHUNK, CHUNK, 2 * DN_HEADS), -1, -2)
    o_dirs = _dn_scan(q, k, v, ab_col, ab_row, wts["dn_par_col"], wts["dn_par_row"], batch, seq)
    o_dn = _dn_post(o_dirs, proj, wts["dn_out_norm"], batch, seq, _tile(seq, 256))

    cos, sin = _rope_tables(seq)
    tp = _tile(seq, 256)
    aq, ak, av = _attn_prep(proj, cos, sin, wts["q_norm"], wts["k_norm"], tp)
    o_at = _flash(aq, ak, av, meta["k"], meta["v"], wts["attn_out_norm"], batch, seq, _tile(seq, 128), _tile(seq, 512))

    h2 = _outproj(o_dn, o_at, wts["w_out"], xf, _tile(m, 512), 1024)
    h2n = _rmsnorm(h2, wts["norm2"], _tile(m, 256))
    s1, s2, a, b, thr = _peer_route(h2n, wts["peer_wq_t"], wts["peer_keys"], _tile(m, 256))
    y = _peer_dense(h2n, h2, wts["peer_u"], wts["peer_v"], s1, s2, a, b, thr, _tile(m, 512), 256)
    return y.reshape(batch, seq, d)


def kernel(x_prompt, x_sample, meta_tokens, norm1, w_in, conv_w, a_log, dt_bias, dn_out_norm, q_norm, k_norm,
           attn_out_norm, w_out, norm2, peer_wq, peer_keys, peer_u, peer_v):
    w = w_in[0]
    offs = [0]
    for n in IN_SPLITS:
        offs.append(offs[-1] + n)
    part = [w[:, offs[i]:offs[i + 1]] for i in range(len(IN_SPLITS))]
    w_big = jnp.concatenate([part[0], part[1], part[2], part[5], part[6], part[7], part[8]], axis=1).astype(BF16)
    w_small = jnp.concatenate(
        [part[3], part[4], jnp.zeros((w.shape[0], LANES - 4 * DN_HEADS), w.dtype)], axis=1).astype(BF16)
    cw = conv_w[0]
    wts = {
        "norm1": norm1[0],
        "w_big": w_big,
        "w_small": w_small,
        "conv_w": cw,
        "dn_par_col": jnp.stack([a_log[0], dt_bias[0]], axis=1),
        "dn_par_row": jnp.stack([a_log[0], dt_bias[0]], axis=2),
        "dn_out_norm": dn_out_norm[0],
        "q_norm": q_norm[0],
        "k_norm": k_norm[0],
        "attn_out_norm": attn_out_norm[0],
        "w_out": w_out[0].astype(BF16),
        "norm2": norm2[0],
        "peer_wq_t": peer_wq[0].T.astype(BF16),
        "peer_keys": peer_keys[0].astype(BF16),
        "peer_u": peer_u[0].astype(BF16),
        "peer_v": peer_v[0].astype(BF16),
    }

    mn = _rmsnorm(meta_tokens, wts["norm1"], N_META)
    proj_meta = _matmul(mn, w_big, N_META, 512)
    ab_meta = _matmul(mn, w_small, N_META, LANES)
    ones = jnp.ones((N_META, HEAD_DIM), F32)
    _, k_meta, v_meta = _attn_prep(proj_meta, ones, jnp.zeros_like(ones), wts["q_norm"], wts["k_norm"], N_META)
    pad = ((0, LANES - N_META), (0, 0))
    meta = {"proj": proj_meta, "ab": ab_meta, "k": jnp.pad(k_meta, pad), "v": jnp.pad(v_meta, pad)}

    return (_encode(x_prompt, meta, wts), _encode(x_sample, meta, wts))
```

```python
import functools

import jax
import jax.numpy as jnp
from jax import lax
from jax.experimental import pallas as pl
from jax.experimental.pallas import tpu as pltpu

F32 = jnp.float32
BF16 = jnp.bfloat16

HEAD_DIM = 128
DN_HEADS = 16
DN_DIM = DN_HEADS * HEAD_DIM
AT_HEADS = 16
AT_KV_HEADS = 4
AT_GROUP = AT_HEADS // AT_KV_HEADS
AT_Q_DIM = AT_HEADS * HEAD_DIM
AT_KV_DIM = AT_KV_HEADS * HEAD_DIM
CONV_K = 5
CHUNK = 64
GRID_W = 64
AXIS_DIM = HEAD_DIM // 2
ROPE_THETA = 10000.0
N_META = 16
PEER_HEADS = 8
N_KEYS = 128
PEER_KEY_DIM = 256
PEER_TOPK = 16
EPS = 1e-6
IN_SPLITS = (DN_DIM, DN_DIM, DN_DIM, 2 * DN_HEADS, 2 * DN_HEADS, DN_DIM, AT_Q_DIM, AT_KV_DIM, AT_KV_DIM)

COL_DQ, COL_DK, COL_DV, COL_GATE, COL_AQ = 0, DN_DIM, 2 * DN_DIM, 3 * DN_DIM, 4 * DN_DIM
COL_AK = COL_AQ + AT_Q_DIM
COL_AV = COL_AK + AT_KV_DIM
BIG_DIM = COL_AV + AT_KV_DIM

LANES = 128
META_PAD = CHUNK - N_META
VMEM_LIMIT = 56 * 1024 * 1024

NT_DIMS = (((1,), (1,)), ((), ()))
TN_DIMS = (((0,), (0,)), ((), ()))


def _cparams(*sem):
    return pltpu.CompilerParams(dimension_semantics=sem, vmem_limit_bytes=VMEM_LIMIT)


def _rmsnorm_kernel(x_ref, w_ref, o_ref):
    x = x_ref[...]
    ms = jnp.mean(x * x, axis=-1, keepdims=True)
    o_ref[...] = (x * lax.rsqrt(ms + EPS) * w_ref[...]).astype(o_ref.dtype)


def _rmsnorm(x, w, tm):
    m, d = x.shape
    return pl.pallas_call(
        _rmsnorm_kernel,
        grid=(m // tm,),
        in_specs=[pl.BlockSpec((tm, d), lambda i: (i, 0)), pl.BlockSpec((1, d), lambda i: (0, 0))],
        out_specs=pl.BlockSpec((tm, d), lambda i: (i, 0)),
        out_shape=jax.ShapeDtypeStruct((m, d), BF16),
        compiler_params=_cparams("parallel"),
        name="rmsnorm",
    )(x, w.reshape(1, d))


def _mm_kernel(x_ref, w_ref, o_ref):
    o_ref[...] = jnp.dot(x_ref[...], w_ref[...], preferred_element_type=F32).astype(o_ref.dtype)


def _matmul(x, w, tm, tn, out_dtype=F32):
    m, k = x.shape
    n = w.shape[1]
    return pl.pallas_call(
        _mm_kernel,
        grid=(m // tm, n // tn),
        in_specs=[pl.BlockSpec((tm, k), lambda i, j: (i, 0)), pl.BlockSpec((k, tn), lambda i, j: (0, j))],
        out_specs=pl.BlockSpec((tm, tn), lambda i, j: (i, j)),
        out_shape=jax.ShapeDtypeStruct((m, n), out_dtype),
        compiler_params=_cparams("parallel", "arbitrary"),
        name="matmul",
    )(x, w)


def _outproj_kernel(a_ref, b_ref, wa_ref, wb_ref, x_ref, o_ref):
    acc = jnp.dot(a_ref[...], wa_ref[...], preferred_element_type=F32)
    acc += jnp.dot(b_ref[...], wb_ref[...], preferred_element_type=F32)
    o_ref[...] = x_ref[...] + acc


def _outproj(o_dn, o_at, w_out, x, tm, tn):
    m, k = o_dn.shape
    n = w_out.shape[1]
    return pl.pallas_call(
        _outproj_kernel,
        grid=(m // tm, n // tn),
        in_specs=[
            pl.BlockSpec((tm, k), lambda i, j: (i, 0)),
            pl.BlockSpec((tm, k), lambda i, j: (i, 0)),
            pl.BlockSpec((k, tn), lambda i, j: (0, j)),
            pl.BlockSpec((k, tn), lambda i, j: (1, j)),
            pl.BlockSpec((tm, tn), lambda i, j: (i, j)),
        ],
        out_specs=pl.BlockSpec((tm, tn), lambda i, j: (i, j)),
        out_shape=jax.ShapeDtypeStruct((m, n), F32),
        compiler_params=_cparams("parallel", "arbitrary"),
        name="outproj",
    )(o_dn, o_at, w_out, w_out, x)


CONV_LEAD = 8 + META_PAD + N_META
CONV_ROWS = 512


def _dn_prep_kernel(xq, xk, xv, mq, mk, mv, wq, wk, wv, oq, ok, ov, buf, *, seq):
    half = CONV_K // 2

    def run(x_ref, m_ref, w_ref, o_ref, normalise, scale):
        buf[0:CONV_LEAD - N_META, :] = jnp.zeros((CONV_LEAD - N_META, LANES), F32)
        buf[CONV_LEAD - N_META:CONV_LEAD, :] = m_ref[...]
        for c in range(seq // CONV_ROWS):
            buf[CONV_LEAD + c * CONV_ROWS:CONV_LEAD + (c + 1) * CONV_ROWS, :] = x_ref[c * CONV_ROWS:(c + 1) * CONV_ROWS, :]
        buf[CONV_LEAD + seq:CONV_LEAD + seq + 8, :] = jnp.zeros((8, LANES), F32)
        w = w_ref[...]

        def act(t0, n):
            y = w[0:1, :] * buf[t0 + 8 - half:t0 + 8 - half + n, :]
            for k in range(1, CONV_K):
                y = y + w[k:k + 1, :] * buf[t0 + 8 - half + k:t0 + 8 - half + k + n, :]
            y = y * jax.nn.sigmoid(y)
            if normalise:
                y = y * lax.rsqrt(jnp.sum(y * y, axis=-1, keepdims=True) + EPS) * scale
            return y.astype(o_ref.dtype)

        o_ref[0:META_PAD, :] = jnp.zeros((META_PAD, LANES), o_ref.dtype)
        o_ref[META_PAD:CHUNK, :] = act(META_PAD, N_META)
        for c in range(seq // CONV_ROWS):
            o_ref[CHUNK + c * CONV_ROWS:CHUNK + (c + 1) * CONV_ROWS, :] = act(CHUNK + c * CONV_ROWS, CONV_ROWS)

    run(xq, mq, wq, oq, True, HEAD_DIM ** -0.5)
    run(xk, mk, wk, ok, True, 1.0)
    run(xv, mv, wv, ov, False, 1.0)


def _dn_prep(proj, proj_meta, conv_w, batch, seq):
    t = seq + CHUNK
    proj3 = proj.reshape(batch, seq, BIG_DIM)
    hq, hk, hv = COL_DQ // LANES, COL_DK // LANES, COL_DV // LANES

    def xspec(off):
        return pl.BlockSpec((None, seq, LANES), lambda b, h: (b, 0, off + h))

    def mspec(off):
        return pl.BlockSpec((N_META, LANES), lambda b, h: (0, off + h))

    def wspec(off):
        return pl.BlockSpec((CONV_K, LANES), lambda b, h: (0, off + h))

    ospec = pl.BlockSpec((None, t, LANES), lambda b, h: (b, 0, h))
    oshape = jax.ShapeDtypeStruct((batch, t, DN_DIM), BF16)
    return pl.pallas_call(
        functools.partial(_dn_prep_kernel, seq=seq),
        grid=(batch, DN_HEADS),
        in_specs=[xspec(hq), xspec(hk), xspec(hv), mspec(hq), mspec(hk), mspec(hv), wspec(hq), wspec(hk), wspec(hv)],
        out_specs=[ospec, ospec, ospec],
        out_shape=[oshape, oshape, oshape],
        scratch_shapes=[pltpu.VMEM((seq + CONV_LEAD + 8, LANES), F32)],
        compiler_params=_cparams("parallel", "parallel"),
        name="dn_prep",
    )(proj3, proj3, proj3, proj_meta, proj_meta, proj_meta, conv_w, conv_w, conv_w)


def _dn_scan_kernel(q_ref, k_ref, v_ref, abc_ref, abr_ref, pc_ref, pr_ref, o_ref, state, o_scr):
    d = pl.program_id(1)
    n = pl.program_id(2)
    n_chunks = pl.num_programs(2)
    fwd = d == 0
    n_eff = jnp.where(fwd, n, n_chunks - 1 - n)

    @pl.when(n == 0)
    def _():
        state[...] = jnp.zeros_like(state)

    row = lax.broadcasted_iota(jnp.int32, (CHUNK, CHUNK), 0)
    col = lax.broadcasted_iota(jnp.int32, (CHUNK, CHUNK), 1)
    order = (row - col) * jnp.where(fwd, 1, -1)
    incl = order >= 0
    strict = order > 0
    eye = (row == col).astype(F32)
    levels = []
    for sh in range(6):
        levels.append(jnp.logical_and((row >> (sh + 1)) == (col >> (sh + 1)), (row >> sh) != (col >> sh)))

    first = n_eff == 0
    t_c = lax.broadcasted_iota(jnp.int32, (CHUNK, 1), 0)
    t_r = lax.broadcasted_iota(jnp.int32, (1, CHUNK), 1)
    live_c = jnp.logical_or(jnp.logical_not(first), t_c >= META_PAD)
    live_r = jnp.logical_or(jnp.logical_not(first), t_r >= META_PAD)

    abc = abc_ref[...]
    abr = abr_ref[...]
    beta_c = jnp.where(live_c, jax.nn.sigmoid(abc[:, 0:DN_HEADS]), 0.0)
    g_c = jnp.where(live_c, -jnp.exp(pc_ref[0:1, :]) * jax.nn.softplus(abc[:, DN_HEADS:] + pc_ref[1:2, :]), 0.0)
    g_r = jnp.where(live_r, -jnp.exp(pr_ref[:, 0:1]) * jax.nn.softplus(abr[DN_HEADS:, :] + pr_ref[:, 1:2]), 0.0)

    tri = incl.astype(F32)
    hi = lax.Precision.HIGHEST
    gcum_c = jnp.dot(tri, g_c, precision=hi, preferred_element_type=F32)
    gcum_r = lax.dot_general(g_r, tri, NT_DIMS, precision=hi, preferred_element_type=F32)
    gtot = jnp.sum(g_c, axis=0, keepdims=True)
    eg = jnp.exp(gcum_c)
    ekd = jnp.exp(gtot - gcum_c)
    cdec = jnp.exp(gtot)

    heads = range(DN_HEADS)
    sls = [slice(h * HEAD_DIM, (h + 1) * HEAD_DIM) for h in heads]

    def dot(x, y):
        return jnp.dot(x, y, preferred_element_type=F32)

    qh = [q_ref[:, sl] for sl in sls]
    kh = [k_ref[:, sl] for sl in sls]
    bc = [beta_c[:, h:h + 1] for h in heads]
    egh = [eg[:, h:h + 1] for h in heads]
    kk = [lax.dot_general(kh[h], kh[h], NT_DIMS, preferred_element_type=F32) for h in heads]
    qk = [lax.dot_general(qh[h], kh[h], NT_DIMS, preferred_element_type=F32) for h in heads]
    decay = [jnp.exp(jnp.where(incl, gcum_c[:, h:h + 1] - gcum_r[h:h + 1, :], -jnp.inf)) for h in heads]
    a = [jnp.where(strict, bc[h] * kk[h] * decay[h], 0.0) for h in heads]
    qkd = [(qk[h] * decay[h]).astype(BF16) for h in heads]
    tinv = [eye - jnp.where(levels[0], a[h], 0.0) for h in heads]
    for lvl in levels[1:]:
        cb = [jnp.where(lvl, a[h], 0.0).astype(BF16) for h in heads]
        tb = [tinv[h].astype(BF16) for h in heads]
        tc = [dot(tb[h], cb[h]).astype(BF16) for h in heads]
        tinv = [tinv[h] - dot(tc[h], tb[h]) for h in heads]
    kf = [kh[h].astype(F32) for h in heads]
    rhs = [jnp.concatenate([v_ref[:, sls[h]].astype(F32) * bc[h], kf[h] * (bc[h] * egh[h])], axis=1).astype(BF16)
           for h in heads]
    sol = [dot(tinv[h].astype(BF16), rhs[h]) for h in heads]
    q_dec = [(qh[h].astype(F32) * egh[h]).astype(BF16) for h in heads]
    k_dec = [(kf[h] * ekd[:, h:h + 1]).astype(BF16) for h in heads]
    s_old = [state[h] for h in heads]
    sb = [s_old[h].astype(BF16) for h in heads]
    wb = [sol[h][:, HEAD_DIM:].astype(BF16) for h in heads]
    vb = [(sol[h][:, :HEAD_DIM] - dot(wb[h], sb[h])).astype(BF16) for h in heads]
    o_inter = [dot(q_dec[h], sb[h]) for h in heads]
    for h in heads:
        o_scr[:, sls[h]] = o_inter[h] + dot(qkd[h], vb[h])
    for h in heads:
        state[h] = s_old[h] * cdec[:, h:h + 1] + lax.dot_general(k_dec[h], vb[h], TN_DIMS, preferred_element_type=F32)

    @pl.when(n_eff > 0)
    def _():
        o_ref[...] = o_scr[...]


def _dn_scan(q, k, v, ab_col, ab_row, par_col, par_row, batch, seq):
    n_chunks = (seq + CHUNK) // CHUNK

    def chunk(d, n):
        return jnp.where(d == 0, n, n_chunks - 1 - n)

    qspec = pl.BlockSpec((None, CHUNK, DN_DIM), lambda b, d, n: (b, chunk(d, n), 0))
    return pl.pallas_call(
        _dn_scan_kernel,
        grid=(batch, 2, n_chunks),
        in_specs=[
            qspec, qspec, qspec,
            pl.BlockSpec((None, None, CHUNK, 2 * DN_HEADS), lambda b, d, n: (b, d, chunk(d, n), 0)),
            pl.BlockSpec((None, None, None, 2 * DN_HEADS, CHUNK), lambda b, d, n: (b, d, chunk(d, n), 0, 0)),
            pl.BlockSpec((None, 2, DN_HEADS), lambda b, d, n: (d, 0, 0)),
            pl.BlockSpec((None, DN_HEADS, 2), lambda b, d, n: (d, 0, 0)),
        ],
        out_specs=pl.BlockSpec((None, None, CHUNK, DN_DIM), lambda b, d, n: (b, d, jnp.maximum(chunk(d, n) - 1, 0), 0)),
        out_shape=jax.ShapeDtypeStruct((batch, 2, seq, DN_DIM), F32),
        scratch_shapes=[pltpu.VMEM((DN_HEADS, HEAD_DIM, HEAD_DIM), F32), pltpu.VMEM((CHUNK, DN_DIM), F32)],
        compiler_params=_cparams("parallel", "arbitrary", "arbitrary"),
        name="dn_scan",
    )(q, k, v, ab_col, ab_row, par_col, par_row)


def _dn_post_kernel(o_ref, g_ref, w_ref, y_ref):
    w = w_ref[...]
    for h in range(DN_HEADS):
        sl = slice(h * HEAD_DIM, (h + 1) * HEAD_DIM)
        o = o_ref[0, :, sl] + o_ref[1, :, sl]
        o = o * lax.rsqrt(jnp.mean(o * o, axis=-1, keepdims=True) + EPS) * w
        g = g_ref[:, sl]
        y_ref[:, sl] = (o * (g * jax.nn.sigmoid(g))).astype(y_ref.dtype)


def _dn_post(o, proj, out_norm, batch, seq, tm):
    nb = seq // tm
    return pl.pallas_call(
        _dn_post_kernel,
        grid=(batch, nb),
        in_specs=[
            pl.BlockSpec((None, 2, tm, DN_DIM), lambda b, i: (b, 0, i, 0)),
            pl.BlockSpec((tm, DN_DIM), lambda b, i: (b * nb + i, COL_GATE // DN_DIM)),
            pl.BlockSpec((1, HEAD_DIM), lambda b, i: (0, 0)),
        ],
        out_specs=pl.BlockSpec((tm, DN_DIM), lambda b, i: (b * nb + i, 0)),
        out_shape=jax.ShapeDtypeStruct((batch * seq, DN_DIM), BF16),
        compiler_params=_cparams("parallel", "parallel"),
        name="dn_post",
    )(o, proj, out_norm.reshape(1, HEAD_DIM))


def _rope_tables(seq):
    rows = seq // GRID_W
    row = jnp.repeat(jnp.arange(rows, dtype=F32), GRID_W)
    col = jnp.tile(jnp.arange(GRID_W, dtype=F32), rows)
    inv_freq = ROPE_THETA ** (-jnp.arange(0, AXIS_DIM, 2, dtype=F32) / AXIS_DIM)
    ar = row[:, None] * inv_freq
    ac = col[:, None] * inv_freq
    ang = jnp.concatenate([ar, ar, ac, ac], -1)
    return jnp.cos(ang), jnp.sin(ang)


def _attn_prep_kernel(q_ref, k_ref, v_ref, cos_ref, sin_ref, qw_ref, kw_ref, qo_ref, ko_ref, vo_ref):
    cos = cos_ref[...]
    sin = sin_ref[...]
    lane = lax.broadcasted_iota(jnp.int32, cos.shape, 1)
    first_half = (lane % (HEAD_DIM // 2)) < (HEAD_DIM // 4)

    def norm_rope(x, w, scale):
        x = x * lax.rsqrt(jnp.mean(x * x, axis=-1, keepdims=True) + EPS) * w
        rot = jnp.where(first_half, -pltpu.roll(x, HEAD_DIM - HEAD_DIM // 4, 1), pltpu.roll(x, HEAD_DIM // 4, 1))
        return (x * cos + rot * sin) * scale

    for h in range(AT_HEADS):
        sl = slice(h * HEAD_DIM, (h + 1) * HEAD_DIM)
        qo_ref[:, sl] = norm_rope(q_ref[:, sl], qw_ref[...], HEAD_DIM ** -0.5).astype(qo_ref.dtype)
    for h in range(AT_KV_HEADS):
        sl = slice(h * HEAD_DIM, (h + 1) * HEAD_DIM)
        ko_ref[:, sl] = norm_rope(k_ref[:, sl], kw_ref[...], 1.0).astype(ko_ref.dtype)
    vo_ref[...] = v_ref[...].astype(vo_ref.dtype)


def _attn_prep(proj, cos, sin, q_norm, k_norm, tm):
    m = proj.shape[0]
    nrep = cos.shape[0] // tm
    return pl.pallas_call(
        _attn_prep_kernel,
        grid=(m // tm,),
        in_specs=[
            pl.BlockSpec((tm, AT_Q_DIM), lambda i: (i, COL_AQ // AT_Q_DIM)),
            pl.BlockSpec((tm, AT_KV_DIM), lambda i: (i, COL_AK // AT_KV_DIM)),
            pl.BlockSpec((tm, AT_KV_DIM), lambda i: (i, COL_AV // AT_KV_DIM)),
            pl.BlockSpec((tm, HEAD_DIM), lambda i: (i % nrep, 0)),
            pl.BlockSpec((tm, HEAD_DIM), lambda i: (i % nrep, 0)),
            pl.BlockSpec((1, HEAD_DIM), lambda i: (0, 0)),
            pl.BlockSpec((1, HEAD_DIM), lambda i: (0, 0)),
        ],
        out_specs=[
            pl.BlockSpec((tm, AT_Q_DIM), lambda i: (i, 0)),
            pl.BlockSpec((tm, AT_KV_DIM), lambda i: (i, 0)),
            pl.BlockSpec((tm, AT_KV_DIM), lambda i: (i, 0)),
        ],
        out_shape=[
            jax.ShapeDtypeStruct((m, AT_Q_DIM), BF16),
            jax.ShapeDtypeStruct((m, AT_KV_DIM), BF16),
            jax.ShapeDtypeStruct((m, AT_KV_DIM), BF16),
        ],
        compiler_params=_cparams("parallel"),
        name="attn_prep",
    )(proj, proj, proj, cos, sin, q_norm.reshape(1, HEAD_DIM), k_norm.reshape(1, HEAD_DIM))


def _flash_kernel(q_ref, k_ref, v_ref, km_ref, vm_ref, w_ref, o_ref, o_scr, *, seq, tq, tk):
    lane = lax.broadcasted_iota(jnp.int32, (1, LANES), 1)
    is_meta = lane < N_META
    n_kv = seq // tk
    for g in range(AT_KV_HEADS):
        ks = slice(g * HEAD_DIM, (g + 1) * HEAD_DIM)
        qg = jnp.concatenate(
            [q_ref[:, (AT_GROUP * g + j) * HEAD_DIM:(AT_GROUP * g + j + 1) * HEAD_DIM] for j in range(AT_GROUP)], axis=0)
        s0 = lax.dot_general(qg, km_ref[:, ks], NT_DIMS, preferred_element_type=F32)
        s0 = jnp.where(is_meta, s0, -jnp.inf)
        m0 = jnp.max(s0, axis=-1, keepdims=True)
        p0 = jnp.exp(s0 - m0)
        l0 = jnp.sum(p0, axis=-1, keepdims=True)
        acc0 = jnp.dot(p0.astype(BF16), vm_ref[:, ks], preferred_element_type=F32)

        def body(j, carry):
            m, l, acc = carry
            start = pl.multiple_of(j * tk, tk)
            kb = k_ref[pl.ds(start, tk), ks]
            vb = v_ref[pl.ds(start, tk), ks]
            s = lax.dot_general(qg, kb, NT_DIMS, preferred_element_type=F32)
            m_new = jnp.maximum(m, jnp.max(s, axis=-1, keepdims=True))
            alpha = jnp.exp(m - m_new)
            p = jnp.exp(s - m_new)
            l = alpha * l + jnp.sum(p, axis=-1, keepdims=True)
            acc = alpha * acc + jnp.dot(p.astype(BF16), vb, preferred_element_type=F32)
            return m_new, l, acc

        _, l, acc = lax.fori_loop(0, n_kv, body, (m0, l0, acc0), unroll=2 if n_kv % 2 == 0 else 1)
        og = acc / l
        for j in range(AT_GROUP):
            h = AT_GROUP * g + j
            o_scr[:, h * HEAD_DIM:(h + 1) * HEAD_DIM] = og[j * tq:(j + 1) * tq, :]
    o = o_scr[...]
    o_ref[...] = (o * lax.rsqrt(jnp.mean(o * o, axis=-1, keepdims=True) + EPS) * w_ref[...]).astype(o_ref.dtype)


def _flash(q, k, v, k_meta, v_meta, out_norm, batch, seq, tq, tk):
    nq = seq // tq
    return pl.pallas_call(
        functools.partial(_flash_kernel, seq=seq, tq=tq, tk=tk),
        grid=(batch, nq),
        in_specs=[
            pl.BlockSpec((tq, AT_Q_DIM), lambda b, i: (b * nq + i, 0)),
            pl.BlockSpec((seq, AT_KV_DIM), lambda b, i: (b, 0)),
            pl.BlockSpec((seq, AT_KV_DIM), lambda b, i: (b, 0)),
            pl.BlockSpec((LANES, AT_KV_DIM), lambda b, i: (0, 0)),
            pl.BlockSpec((LANES, AT_KV_DIM), lambda b, i: (0, 0)),
            pl.BlockSpec((1, AT_Q_DIM), lambda b, i: (0, 0)),
        ],
        out_specs=pl.BlockSpec((tq, AT_Q_DIM), lambda b, i: (b * nq + i, 0)),
        out_shape=jax.ShapeDtypeStruct((batch * seq, AT_Q_DIM), BF16),
        scratch_shapes=[pltpu.VMEM((tq, AT_Q_DIM), F32)],
        compiler_params=_cparams("parallel", "arbitrary"),
        name="flash",
    )(q, k, v, k_meta, v_meta, out_norm.reshape(1, AT_Q_DIM))


def _top_desc(x, count):
    rows = x.shape[0]
    ridx = lax.broadcasted_iota(jnp.int32, x.shape, 0).astype(F32)
    out = []
    for _ in range(count):
        m = jnp.max(x, axis=0, keepdims=True)
        out.append(m)
        first = jnp.min(jnp.where(x == m, ridx, float(rows)), axis=0, keepdims=True)
        x = jnp.where(ridx == first, -jnp.inf, x)
    return out


def _peer_route_kernel(h_ref, wq_ref, keys_ref, s1_ref, s2_ref, a_ref, b_ref, thr_ref):
    q_t = lax.dot_general(wq_ref[...], h_ref[...], NT_DIMS, preferred_element_type=F32)
    half = PEER_KEY_DIM // 2
    s1 = jnp.dot(keys_ref[0], q_t[:half].astype(BF16), preferred_element_type=F32)
    s2 = jnp.dot(keys_ref[1], q_t[half:].astype(BF16), preferred_element_type=F32)
    t1 = _top_desc(s1, PEER_TOPK)
    t2 = _top_desc(s2, PEER_TOPK)
    t2cat = jnp.concatenate(t2, axis=0)
    t1cat = jnp.concatenate(t1, axis=0)
    oct_ = PEER_TOPK // 2
    cand = jnp.concatenate(
        [t1[0] + t2cat] + [t1[i] + t2cat[:oct_] for i in range(1, oct_)] + [t1cat[oct_:] + t2[0]], axis=0)
    top = _top_desc(cand, PEER_TOPK)
    z = jnp.exp(top[0] - top[0])
    for i in range(1, PEER_TOPK):
        z = z + jnp.exp(top[i] - top[0])
    s1_ref[...] = s1
    s2_ref[...] = s2
    a_ref[...] = jnp.exp(s1 - t1[0]) / z
    b_ref[...] = jnp.exp(s2 - t2[0])
    thr_ref[...] = top[PEER_TOPK - 1]


def _peer_route(hn, wq_t, keys, tm):
    m, d = hn.shape
    spec = pl.BlockSpec((None, N_KEYS, tm), lambda i, h: (h, 0, i))
    shape = jax.ShapeDtypeStruct((PEER_HEADS, N_KEYS, m), F32)
    return pl.pallas_call(
        _peer_route_kernel,
        grid=(m // tm, PEER_HEADS),
        in_specs=[
            pl.BlockSpec((tm, d), lambda i, h: (i, 0)),
            pl.BlockSpec((PEER_KEY_DIM, d), lambda i, h: (h, 0)),
            pl.BlockSpec((None, 2, N_KEYS, PEER_KEY_DIM // 2), lambda i, h: (h, 0, 0, 0)),
        ],
        out_specs=[spec, spec, spec, spec, pl.BlockSpec((None, 1, tm), lambda i, h: (h, 0, i))],
        out_shape=[shape, shape, shape, shape, jax.ShapeDtypeStruct((PEER_HEADS, 1, m), F32)],
        compiler_params=_cparams("parallel", "arbitrary"),
        name="peer_route",
    )(hn, wq_t, keys)


def _peer_act_kernel(hn_ref, u_ref, s1_ref, s2_ref, a_ref, b_ref, thr_ref, act_ref, *, te):
    j = pl.program_id(1)
    s_t = lax.dot_general(u_ref[...], hn_ref[...], NT_DIMS, preferred_element_type=F32)
    per = te // N_KEYS
    for ii in range(per):
        i = j * per + ii
        gate = None
        for h in range(PEER_HEADS):
            score = s1_ref[h, pl.ds(i, 1), :] + s2_ref[h]
            term = jnp.where(score >= thr_ref[h], b_ref[h], 0.0) * a_ref[h, pl.ds(i, 1), :]
            gate = term if gate is None else gate + term
        x = s_t[ii * N_KEYS:(ii + 1) * N_KEYS, :]
        act = 0.5 * x * (1.0 + lax.erf(x * (2.0 ** -0.5))) * gate
        act_ref[ii * N_KEYS:(ii + 1) * N_KEYS, :] = act.astype(act_ref.dtype)


def _peer_act(hn, u, s1, s2, a, b, thr, tm, te):
    m, d = hn.shape
    n_exp = u.shape[0]
    once = pl.Buffered(1)
    tok = pl.BlockSpec((PEER_HEADS, N_KEYS, tm), lambda i, j: (0, 0, i), pipeline_mode=once)
    return pl.pallas_call(
        functools.partial(_peer_act_kernel, te=te),
        grid=(m // tm, n_exp // te),
        in_specs=[
            pl.BlockSpec((tm, d), lambda i, j: (i, 0), pipeline_mode=once),
            pl.BlockSpec((te, d), lambda i, j: (j, 0)),
            tok, tok, tok, tok,
            pl.BlockSpec((PEER_HEADS, 1, tm), lambda i, j: (0, 0, i), pipeline_mode=once),
        ],
        out_specs=pl.BlockSpec((te, tm), lambda i, j: (j, i)),
        out_shape=jax.ShapeDtypeStruct((n_exp, m), BF16),
        compiler_params=_cparams("parallel", "arbitrary"),
        name="peer_act",
    )(hn, u, s1, s2, a, b, thr)


def _peer_out_kernel(act_ref, v_ref, res_ref, o_ref):
    k = pl.program_id(1)

    @pl.when(k == 0)
    def _():
        o_ref[...] = res_ref[...]

    o_ref[...] += lax.dot_general(act_ref[...], v_ref[...], TN_DIMS, preferred_element_type=F32)


def _peer_out(act_t, v, res, tm, tk):
    n_exp, m = act_t.shape
    d = v.shape[1]
    once = pl.Buffered(1)
    return pl.pallas_call(
        _peer_out_kernel,
        grid=(m // tm, n_exp // tk),
        in_specs=[
            pl.BlockSpec((tk, tm), lambda i, k: (k, i)),
            pl.BlockSpec((tk, d), lambda i, k: (k, 0)),
            pl.BlockSpec((tm, d), lambda i, k: (i, 0), pipeline_mode=once),
        ],
        out_specs=pl.BlockSpec((tm, d), lambda i, k: (i, 0), pipeline_mode=once),
        out_shape=jax.ShapeDtypeStruct((m, d), F32),
        compiler_params=_cparams("parallel", "arbitrary"),
        name="peer_out",
    )(act_t, v, res)


def _tile(m, pref):
    return pref if m % pref == 0 else m


def _encode(x, meta, wts):
    batch, seq, d = x.shape
    m = batch * seq
    xf = x.reshape(m, d)

    xn = _rmsnorm(xf, wts["norm1"], _tile(m, 256))
    proj = _matmul(xn, wts["w_big"], _tile(m, 1024), 512)
    ab = _matmul(xn, wts["w_small"], _tile(m, 1024), LANES)
    proj_meta, ab_meta = meta["proj"], meta["ab"]

    q, k, v = _dn_prep(proj, proj_meta, wts["conv_w"], batch, seq)
    t = seq + CHUNK
    ab_seq = jnp.concatenate([
        jnp.zeros((batch, META_PAD, 4 * DN_HEADS), F32),
        jnp.broadcast_to(ab_meta[None, :, :4 * DN_HEADS], (batch, N_META, 4 * DN_HEADS)),
        ab[:, :4 * DN_HEADS].reshape(batch, seq, 4 * DN_HEADS)], axis=1)
    ab_seq = ab_seq.reshape(batch, t, 2, 2, DN_HEADS)
    ab_col = jnp.transpose(ab_seq, (0, 3, 1, 2, 4)).reshape(batch, 2, t, 2 * DN_HEADS)
    ab_row = jnp.swapaxes(ab_col.reshape(batch, 2, t // CHUNK, CHUNK, 2 * DN_HEADS), -1, -2)
    o_dirs = _dn_scan(q, k, v, ab_col, ab_row, wts["dn_par_col"], wts["dn_par_row"], batch, seq)
    o_dn = _dn_post(o_dirs, proj, wts["dn_out_norm"], batch, seq, _tile(seq, 256))

    cos, sin = _rope_tables(seq)
    tp = _tile(seq, 256)
    aq, ak, av = _attn_prep(proj, cos, sin, wts["q_norm"], wts["k_norm"], tp)
    o_at = _flash(aq, ak, av, meta["k"], meta["v"], wts["attn_out_norm"], batch, seq, _tile(seq, 128), _tile(seq, 1024))

    h2 = _outproj(o_dn, o_at, wts["w_out"], xf, _tile(m, 512), 1024)
    h2n = _rmsnorm(h2, wts["norm2"], _tile(m, 256))
    s1, s2, a, b, thr = _peer_route(h2n, wts["peer_wq_t"], wts["peer_keys"], _tile(m, 256))
    act_t = _peer_act(h2n, wts["peer_u"], s1, s2, a, b, thr, _tile(m, 512), 512)
    y = _peer_out(act_t, wts["peer_v"], h2, _tile(m, 512), 1024)
    return y.reshape(batch, seq, d)


def kernel(x_prompt, x_sample, meta_tokens, norm1, w_in, conv_w, a_log, dt_bias, dn_out_norm, q_norm, k_norm,
           attn_out_norm, w_out, norm2, peer_wq, peer_keys, peer_u, peer_v):
    w = w_in[0]
    offs = [0]
    for n in IN_SPLITS:
        offs.append(offs[-1] + n)
    part = [w[:, offs[i]:offs[i + 1]] for i in range(len(IN_SPLITS))]
    w_big = jnp.concatenate([part[0], part[1], part[2], part[5], part[6], part[7], part[8]], axis=1).astype(BF16)
    w_small = jnp.concatenate(
        [part[3], part[4], jnp.zeros((w.shape[0], LANES - 4 * DN_HEADS), w.dtype)], axis=1).astype(BF16)
    cw = conv_w[0]
    wts = {
        "norm1": norm1[0],
        "w_big": w_big,
        "w_small": w_small,
        "conv_w": cw,
        "dn_par_col": jnp.stack([a_log[0], dt_bias[0]], axis=1),
        "dn_par_row": jnp.stack([a_log[0], dt_bias[0]], axis=2),
        "dn_out_norm": dn_out_norm[0],
        "q_norm": q_norm[0],
        "k_norm": k_norm[0],
        "attn_out_norm": attn_out_norm[0],
        "w_out": w_out[0].astype(BF16),
        "norm2": norm2[0],
        "peer_wq_t": peer_wq[0].T.astype(BF16),
        "peer_keys": peer_keys[0].astype(BF16),
        "peer_u": peer_u[0].astype(BF16),
        "peer_v": peer_v[0].astype(BF16),
    }

    mn = _rmsnorm(meta_tokens, wts["norm1"], N_META)
    proj_meta = _matmul(mn, w_big, N_META, 512)
    ab_meta = _matmul(mn, w_small, N_META, LANES)
    ones = jnp.ones((N_META, HEAD_DIM), F32)
    _, k_meta, v_meta = _attn_prep(proj_meta, ones, jnp.zeros_like(ones), wts["q_norm"], wts["k_norm"], N_META)
    pad = ((0, LANES - N_META), (0, 0))
    meta = {"proj": proj_meta, "ab": ab_meta, "k": jnp.pad(k_meta, pad), "v": jnp.pad(v_meta, pad)}

    return (_encode(x_prompt, meta, wts), _encode(x_sample, meta, wts))
```

```python
import functools

import jax
import jax.numpy as jnp
from jax import lax
from jax.experimental import pallas as pl
from jax.experimental.pallas import tpu as pltpu

F32 = jnp.float32
BF16 = jnp.bfloat16

HEAD_DIM = 128
DN_HEADS = 16
DN_DIM = DN_HEADS * HEAD_DIM
AT_HEADS = 16
AT_KV_HEADS = 4
AT_GROUP = AT_HEADS // AT_KV_HEADS
AT_Q_DIM = AT_HEADS * HEAD_DIM
AT_KV_DIM = AT_KV_HEADS * HEAD_DIM
CONV_K = 5
CHUNK = 64
GRID_W = 64
AXIS_DIM = HEAD_DIM // 2
ROPE_THETA = 10000.0
N_META = 16
PEER_HEADS = 8
N_KEYS = 128
PEER_KEY_DIM = 256
PEER_TOPK = 16
EPS = 1e-6
IN_SPLITS = (DN_DIM, DN_DIM, DN_DIM, 2 * DN_HEADS, 2 * DN_HEADS, DN_DIM, AT_Q_DIM, AT_KV_DIM, AT_KV_DIM)

COL_DQ, COL_DK, COL_DV, COL_GATE, COL_AQ = 0, DN_DIM, 2 * DN_DIM, 3 * DN_DIM, 4 * DN_DIM
COL_AK = COL_AQ + AT_Q_DIM
COL_AV = COL_AK + AT_KV_DIM
BIG_DIM = COL_AV + AT_KV_DIM

LANES = 128
META_PAD = CHUNK - N_META
VMEM_LIMIT = 56 * 1024 * 1024

NT_DIMS = (((1,), (1,)), ((), ()))
TN_DIMS = (((0,), (0,)), ((), ()))


def _cparams(*sem):
    return pltpu.CompilerParams(dimension_semantics=sem, vmem_limit_bytes=VMEM_LIMIT)


def _rmsnorm_kernel(x_ref, w_ref, o_ref):
    x = x_ref[...]
    ms = jnp.mean(x * x, axis=-1, keepdims=True)
    o_ref[...] = (x * lax.rsqrt(ms + EPS) * w_ref[...]).astype(o_ref.dtype)


def _rmsnorm(x, w, tm):
    m, d = x.shape
    return pl.pallas_call(
        _rmsnorm_kernel,
        grid=(m // tm,),
        in_specs=[pl.BlockSpec((tm, d), lambda i: (i, 0)), pl.BlockSpec((1, d), lambda i: (0, 0))],
        out_specs=pl.BlockSpec((tm, d), lambda i: (i, 0)),
        out_shape=jax.ShapeDtypeStruct((m, d), BF16),
        compiler_params=_cparams("parallel"),
        name="rmsnorm",
    )(x, w.reshape(1, d))


def _mm_kernel(x_ref, w_ref, o_ref):
    o_ref[...] = jnp.dot(x_ref[...], w_ref[...], preferred_element_type=F32).astype(o_ref.dtype)


def _matmul(x, w, tm, tn, out_dtype=F32):
    m, k = x.shape
    n = w.shape[1]
    return pl.pallas_call(
        _mm_kernel,
        grid=(m // tm, n // tn),
        in_specs=[pl.BlockSpec((tm, k), lambda i, j: (i, 0)), pl.BlockSpec((k, tn), lambda i, j: (0, j))],
        out_specs=pl.BlockSpec((tm, tn), lambda i, j: (i, j)),
        out_shape=jax.ShapeDtypeStruct((m, n), out_dtype),
        compiler_params=_cparams("parallel", "arbitrary"),
        name="matmul",
    )(x, w)


def _outproj_kernel(a_ref, b_ref, wa_ref, wb_ref, x_ref, o_ref):
    acc = jnp.dot(a_ref[...], wa_ref[...], preferred_element_type=F32)
    acc += jnp.dot(b_ref[...], wb_ref[...], preferred_element_type=F32)
    o_ref[...] = x_ref[...] + acc


def _outproj(o_dn, o_at, w_out, x, tm, tn):
    m, k = o_dn.shape
    n = w_out.shape[1]
    return pl.pallas_call(
        _outproj_kernel,
        grid=(m // tm, n // tn),
        in_specs=[
            pl.BlockSpec((tm, k), lambda i, j: (i, 0)),
            pl.BlockSpec((tm, k), lambda i, j: (i, 0)),
            pl.BlockSpec((k, tn), lambda i, j: (0, j)),
            pl.BlockSpec((k, tn), lambda i, j: (1, j)),
            pl.BlockSpec((tm, tn), lambda i, j: (i, j)),
        ],
        out_specs=pl.BlockSpec((tm, tn), lambda i, j: (i, j)),
        out_shape=jax.ShapeDtypeStruct((m, n), F32),
        compiler_params=_cparams("parallel", "arbitrary"),
        name="outproj",
    )(o_dn, o_at, w_out, w_out, x)


CONV_LEAD = 8 + META_PAD + N_META
CONV_ROWS = 512


def _dn_prep_kernel(xq, xk, xv, mq, mk, mv, wq, wk, wv, oq, ok, ov, buf, *, seq):
    half = CONV_K // 2

    def run(x_ref, m_ref, w_ref, o_ref, normalise, scale):
        buf[0:CONV_LEAD - N_META, :] = jnp.zeros((CONV_LEAD - N_META, LANES), F32)
        buf[CONV_LEAD - N_META:CONV_LEAD, :] = m_ref[...]
        for c in range(seq // CONV_ROWS):
            buf[CONV_LEAD + c * CONV_ROWS:CONV_LEAD + (c + 1) * CONV_ROWS, :] = x_ref[c * CONV_ROWS:(c + 1) * CONV_ROWS, :]
        buf[CONV_LEAD + seq:CONV_LEAD + seq + 8, :] = jnp.zeros((8, LANES), F32)
        w = w_ref[...]

        def act(t0, n):
            y = w[0:1, :] * buf[t0 + 8 - half:t0 + 8 - half + n, :]
            for k in range(1, CONV_K):
                y = y + w[k:k + 1, :] * buf[t0 + 8 - half + k:t0 + 8 - half + k + n, :]
            y = y * jax.nn.sigmoid(y)
            if normalise:
                y = y * lax.rsqrt(jnp.sum(y * y, axis=-1, keepdims=True) + EPS) * scale
            return y.astype(o_ref.dtype)

        o_ref[0:META_PAD, :] = jnp.zeros((META_PAD, LANES), o_ref.dtype)
        o_ref[META_PAD:CHUNK, :] = act(META_PAD, N_META)
        for c in range(seq // CONV_ROWS):
            o_ref[CHUNK + c * CONV_ROWS:CHUNK + (c + 1) * CONV_ROWS, :] = act(CHUNK + c * CONV_ROWS, CONV_ROWS)

    run(xq, mq, wq, oq, True, HEAD_DIM ** -0.5)
    run(xk, mk, wk, ok, True, 1.0)
    run(xv, mv, wv, ov, False, 1.0)


def _dn_prep(proj, proj_meta, conv_w, batch, seq):
    t = seq + CHUNK
    proj3 = proj.reshape(batch, seq, BIG_DIM)
    hq, hk, hv = COL_DQ // LANES, COL_DK // LANES, COL_DV // LANES

    def xspec(off):
        return pl.BlockSpec((None, seq, LANES), lambda b, h: (b, 0, off + h))

    def mspec(off):
        return pl.BlockSpec((N_META, LANES), lambda b, h: (0, off + h))

    def wspec(off):
        return pl.BlockSpec((CONV_K, LANES), lambda b, h: (0, off + h))

    ospec = pl.BlockSpec((None, t, LANES), lambda b, h: (b, 0, h))
    oshape = jax.ShapeDtypeStruct((batch, t, DN_DIM), BF16)
    return pl.pallas_call(
        functools.partial(_dn_prep_kernel, seq=seq),
        grid=(batch, DN_HEADS),
        in_specs=[xspec(hq), xspec(hk), xspec(hv), mspec(hq), mspec(hk), mspec(hv), wspec(hq), wspec(hk), wspec(hv)],
        out_specs=[ospec, ospec, ospec],
        out_shape=[oshape, oshape, oshape],
        scratch_shapes=[pltpu.VMEM((seq + CONV_LEAD + 8, LANES), F32)],
        compiler_params=_cparams("parallel", "parallel"),
        name="dn_prep",
    )(proj3, proj3, proj3, proj_meta, proj_meta, proj_meta, conv_w, conv_w, conv_w)


def _dn_scan_kernel(q_ref, k_ref, v_ref, abc_ref, abr_ref, pc_ref, pr_ref, o_ref, state, o_scr):
    d = pl.program_id(1)
    n = pl.program_id(2)
    n_chunks = pl.num_programs(2)
    fwd = d == 0
    n_eff = jnp.where(fwd, n, n_chunks - 1 - n)

    @pl.when(n == 0)
    def _():
        state[...] = jnp.zeros_like(state)

    row = lax.broadcasted_iota(jnp.int32, (CHUNK, CHUNK), 0)
    col = lax.broadcasted_iota(jnp.int32, (CHUNK, CHUNK), 1)
    order = (row - col) * jnp.where(fwd, 1, -1)
    incl = order >= 0
    strict = order > 0
    eye = (row == col).astype(F32)
    levels = []
    for sh in range(6):
        levels.append(jnp.logical_and((row >> (sh + 1)) == (col >> (sh + 1)), (row >> sh) != (col >> sh)))

    first = n_eff == 0
    t_c = lax.broadcasted_iota(jnp.int32, (CHUNK, 1), 0)
    t_r = lax.broadcasted_iota(jnp.int32, (1, CHUNK), 1)
    live_c = jnp.logical_or(jnp.logical_not(first), t_c >= META_PAD)
    live_r = jnp.logical_or(jnp.logical_not(first), t_r >= META_PAD)

    abc = abc_ref[...]
    abr = abr_ref[...]
    beta_c = jnp.where(live_c, jax.nn.sigmoid(abc[:, 0:DN_HEADS]), 0.0)
    g_c = jnp.where(live_c, -jnp.exp(pc_ref[0:1, :]) * jax.nn.softplus(abc[:, DN_HEADS:] + pc_ref[1:2, :]), 0.0)
    g_r = jnp.where(live_r, -jnp.exp(pr_ref[:, 0:1]) * jax.nn.softplus(abr[DN_HEADS:, :] + pr_ref[:, 1:2]), 0.0)

    tri = incl.astype(F32)
    hi = lax.Precision.HIGHEST
    gcum_c = jnp.dot(tri, g_c, precision=hi, preferred_element_type=F32)
    gcum_r = lax.dot_general(g_r, tri, NT_DIMS, precision=hi, preferred_element_type=F32)
    gtot = jnp.sum(g_c, axis=0, keepdims=True)
    eg = jnp.exp(gcum_c)
    ekd = jnp.exp(gtot - gcum_c)
    cdec = jnp.exp(gtot)

    heads = range(DN_HEADS)
    sls = [slice(h * HEAD_DIM, (h + 1) * HEAD_DIM) for h in heads]

    def dot(x, y):
        return jnp.dot(x, y, preferred_element_type=F32)

    qh = [q_ref[:, sl] for sl in sls]
    kh = [k_ref[:, sl] for sl in sls]
    bc = [beta_c[:, h:h + 1] for h in heads]
    egh = [eg[:, h:h + 1] for h in heads]
    kk = [lax.dot_general(kh[h], kh[h], NT_DIMS, preferred_element_type=F32) for h in heads]
    qk = [lax.dot_general(qh[h], kh[h], NT_DIMS, preferred_element_type=F32) for h in heads]
    decay = [jnp.exp(jnp.where(incl, gcum_c[:, h:h + 1] - gcum_r[h:h + 1, :], -jnp.inf)) for h in heads]
    a = [jnp.where(strict, bc[h] * kk[h] * decay[h], 0.0) for h in heads]
    qkd = [(qk[h] * decay[h]).astype(BF16) for h in heads]
    tinv = [eye - jnp.where(levels[0], a[h], 0.0) for h in heads]
    for lvl in levels[1:]:
        cb = [jnp.where(lvl, a[h], 0.0).astype(BF16) for h in heads]
        tb = [tinv[h].astype(BF16) for h in heads]
        tc = [dot(tb[h], cb[h]).astype(BF16) for h in heads]
        tinv = [tinv[h] - dot(tc[h], tb[h]) for h in heads]
    kf = [kh[h].astype(F32) for h in heads]
    rhs = [jnp.concatenate([v_ref[:, sls[h]].astype(F32) * bc[h], kf[h] * (bc[h] * egh[h])], axis=1).astype(BF16)
           for h in heads]
    sol = [dot(tinv[h].astype(BF16), rhs[h]) for h in heads]
    q_dec = [(qh[h].astype(F32) * egh[h]).astype(BF16) for h in heads]
    k_dec = [(kf[h] * ekd[:, h:h + 1]).astype(BF16) for h in heads]
    s_old = [state[h] for h in heads]
    sb = [s_old[h].astype(BF16) for h in heads]
    wb = [sol[h][:, HEAD_DIM:].astype(BF16) for h in heads]
    vb = [(sol[h][:, :HEAD_DIM] - dot(wb[h], sb[h])).astype(BF16) for h in heads]
    o_inter = [dot(q_dec[h], sb[h]) for h in heads]
    for h in heads:
        o_scr[:, sls[h]] = o_inter[h] + dot(qkd[h], vb[h])
    for h in heads:
        state[h] = s_old[h] * cdec[:, h:h + 1] + lax.dot_general(k_dec[h], vb[h], TN_DIMS, preferred_element_type=F32)

    @pl.when(n_eff > 0)
    def _():
        o_ref[...] = o_scr[...]


def _dn_scan(q, k, v, ab_col, ab_row, par_col, par_row, batch, seq):
    n_chunks = (seq + CHUNK) // CHUNK

    def chunk(d, n):
        return jnp.where(d == 0, n, n_chunks - 1 - n)

    qspec = pl.BlockSpec((None, CHUNK, DN_DIM), lambda b, d, n: (b, chunk(d, n), 0))
    return pl.pallas_call(
        _dn_scan_kernel,
        grid=(batch, 2, n_chunks),
        in_specs=[
            qspec, qspec, qspec,
            pl.BlockSpec((None, None, CHUNK, 2 * DN_HEADS), lambda b, d, n: (b, d, chunk(d, n), 0)),
            pl.BlockSpec((None, None, None, 2 * DN_HEADS, CHUNK), lambda b, d, n: (b, d, chunk(d, n), 0, 0)),
            pl.BlockSpec((None, 2, DN_HEADS), lambda b, d, n: (d, 0, 0)),
            pl.BlockSpec((None, DN_HEADS, 2), lambda b, d, n: (d, 0, 0)),
        ],
        out_specs=pl.BlockSpec((None, None, CHUNK, DN_DIM), lambda b, d, n: (b, d, jnp.maximum(chunk(d, n) - 1, 0), 0)),
        out_shape=jax.ShapeDtypeStruct((batch, 2, seq, DN_DIM), F32),
        scratch_shapes=[pltpu.VMEM((DN_HEADS, HEAD_DIM, HEAD_DIM), F32), pltpu.VMEM((CHUNK, DN_DIM), F32)],
        compiler_params=_cparams("parallel", "arbitrary", "arbitrary"),
        name="dn_scan",
    )(q, k, v, ab_col, ab_row, par_col, par_row)


def _dn_post_kernel(o_ref, g_ref, w_ref, y_ref):
    w = w_ref[...]
    for h in range(DN_HEADS):
        sl = slice(h * HEAD_DIM, (h + 1) * HEAD_DIM)
        o = o_ref[0, :, sl] + o_ref[1, :, sl]
        o = o * lax.rsqrt(jnp.mean(o * o, axis=-1, keepdims=True) + EPS) * w
        g = g_ref[:, sl]
        y_ref[:, sl] = (o * (g * jax.nn.sigmoid(g))).astype(y_ref.dtype)


def _dn_post(o, proj, out_norm, batch, seq, tm):
    nb = seq // tm
    return pl.pallas_call(
        _dn_post_kernel,
        grid=(batch, nb),
        in_specs=[
            pl.BlockSpec((None, 2, tm, DN_DIM), lambda b, i: (b, 0, i, 0)),
            pl.BlockSpec((tm, DN_DIM), lambda b, i: (b * nb + i, COL_GATE // DN_DIM)),
            pl.BlockSpec((1, HEAD_DIM), lambda b, i: (0, 0)),
        ],
        out_specs=pl.BlockSpec((tm, DN_DIM), lambda b, i: (b * nb + i, 0)),
        out_shape=jax.ShapeDtypeStruct((batch * seq, DN_DIM), BF16),
        compiler_params=_cparams("parallel", "parallel"),
        name="dn_post",
    )(o, proj, out_norm.reshape(1, HEAD_DIM))


def _rope_tables(seq):
    rows = seq // GRID_W
    row = jnp.repeat(jnp.arange(rows, dtype=F32), GRID_W)
    col = jnp.tile(jnp.arange(GRID_W, dtype=F32), rows)
    inv_freq = ROPE_THETA ** (-jnp.arange(0, AXIS_DIM, 2, dtype=F32) / AXIS_DIM)
    ar = row[:, None] * inv_freq
    ac = col[:, None] * inv_freq
    ang = jnp.concatenate([ar, ar, ac, ac], -1)
    return jnp.cos(ang), jnp.sin(ang)


def _attn_prep_kernel(q_ref, k_ref, v_ref, cos_ref, sin_ref, qw_ref, kw_ref, qo_ref, ko_ref, vo_ref):
    cos = cos_ref[...]
    sin = sin_ref[...]
    lane = lax.broadcasted_iota(jnp.int32, cos.shape, 1)
    first_half = (lane % (HEAD_DIM // 2)) < (HEAD_DIM // 4)

    def norm_rope(x, w, scale):
        x = x * lax.rsqrt(jnp.mean(x * x, axis=-1, keepdims=True) + EPS) * w
        rot = jnp.where(first_half, -pltpu.roll(x, HEAD_DIM - HEAD_DIM // 4, 1), pltpu.roll(x, HEAD_DIM // 4, 1))
        return (x * cos + rot * sin) * scale

    for h in range(AT_HEADS):
        sl = slice(h * HEAD_DIM, (h + 1) * HEAD_DIM)
        qo_ref[:, sl] = norm_rope(q_ref[:, sl], qw_ref[...], HEAD_DIM ** -0.5).astype(qo_ref.dtype)
    for h in range(AT_KV_HEADS):
        sl = slice(h * HEAD_DIM, (h + 1) * HEAD_DIM)
        ko_ref[:, sl] = norm_rope(k_ref[:, sl], kw_ref[...], 1.0).astype(ko_ref.dtype)
    vo_ref[...] = v_ref[...].astype(vo_ref.dtype)


def _attn_prep(proj, cos, sin, q_norm, k_norm, tm):
    m = proj.shape[0]
    nrep = cos.shape[0] // tm
    return pl.pallas_call(
        _attn_prep_kernel,
        grid=(m // tm,),
        in_specs=[
            pl.BlockSpec((tm, AT_Q_DIM), lambda i: (i, COL_AQ // AT_Q_DIM)),
            pl.BlockSpec((tm, AT_KV_DIM), lambda i: (i, COL_AK // AT_KV_DIM)),
            pl.BlockSpec((tm, AT_KV_DIM), lambda i: (i, COL_AV // AT_KV_DIM)),
            pl.BlockSpec((tm, HEAD_DIM), lambda i: (i % nrep, 0)),
            pl.BlockSpec((tm, HEAD_DIM), lambda i: (i % nrep, 0)),
            pl.BlockSpec((1, HEAD_DIM), lambda i: (0, 0)),
            pl.BlockSpec((1, HEAD_DIM), lambda i: (0, 0)),
        ],
        out_specs=[
            pl.BlockSpec((tm, AT_Q_DIM), lambda i: (i, 0)),
            pl.BlockSpec((tm, AT_KV_DIM), lambda i: (i, 0)),
            pl.BlockSpec((tm, AT_KV_DIM), lambda i: (i, 0)),
        ],
        out_shape=[
            jax.ShapeDtypeStruct((m, AT_Q_DIM), BF16),
            jax.ShapeDtypeStruct((m, AT_KV_DIM), BF16),
            jax.ShapeDtypeStruct((m, AT_KV_DIM), BF16),
        ],
        compiler_params=_cparams("parallel"),
        name="attn_prep",
    )(proj, proj, proj, cos, sin, q_norm.reshape(1, HEAD_DIM), k_norm.reshape(1, HEAD_DIM))


def _flash_kernel(q_ref, k_ref, v_ref, km_ref, vm_ref, w_ref, o_ref, o_scr, *, seq, tq, tk):
    lane = lax.broadcasted_iota(jnp.int32, (1, LANES), 1)
    is_meta = lane < N_META
    n_kv = seq // tk
    for g in range(AT_KV_HEADS):
        ks = slice(g * HEAD_DIM, (g + 1) * HEAD_DIM)
        qg = jnp.concatenate(
            [q_ref[:, (AT_GROUP * g + j) * HEAD_DIM:(AT_GROUP * g + j + 1) * HEAD_DIM] for j in range(AT_GROUP)], axis=0)
        s0 = lax.dot_general(qg, km_ref[:, ks], NT_DIMS, preferred_element_type=F32)
        s0 = jnp.where(is_meta, s0, -jnp.inf)
        m0 = jnp.max(s0, axis=-1, keepdims=True)
        p0 = jnp.exp(s0 - m0)
        l0 = jnp.sum(p0, axis=-1, keepdims=True)
        acc0 = jnp.dot(p0.astype(BF16), vm_ref[:, ks], preferred_element_type=F32)

        def body(j, carry):
            m, l, acc = carry
            start = pl.multiple_of(j * tk, tk)
            kb = k_ref[pl.ds(start, tk), ks]
            vb = v_ref[pl.ds(start, tk), ks]
            s = lax.dot_general(qg, kb, NT_DIMS, preferred_element_type=F32)
            m_new = jnp.maximum(m, jnp.max(s, axis=-1, keepdims=True))
            alpha = jnp.exp(m - m_new)
            p = jnp.exp(s - m_new)
            l = alpha * l + jnp.sum(p, axis=-1, keepdims=True)
            acc = alpha * acc + jnp.dot(p.astype(BF16), vb, preferred_element_type=F32)
            return m_new, l, acc

        _, l, acc = lax.fori_loop(0, n_kv, body, (m0, l0, acc0), unroll=2 if n_kv % 2 == 0 else 1)
        og = acc / l
        for j in range(AT_GROUP):
            h = AT_GROUP * g + j
            o_scr[:, h * HEAD_DIM:(h + 1) * HEAD_DIM] = og[j * tq:(j + 1) * tq, :]
    o = o_scr[...]
    o_ref[...] = (o * lax.rsqrt(jnp.mean(o * o, axis=-1, keepdims=True) + EPS) * w_ref[...]).astype(o_ref.dtype)


def _flash(q, k, v, k_meta, v_meta, out_norm, batch, seq, tq, tk):
    nq = seq // tq
    return pl.pallas_call(
        functools.partial(_flash_kernel, seq=seq, tq=tq, tk=tk),
        grid=(batch, nq),
        in_specs=[
            pl.BlockSpec((tq, AT_Q_DIM), lambda b, i: (b * nq + i, 0)),
            pl.BlockSpec((seq, AT_KV_DIM), lambda b, i: (b, 0)),
            pl.BlockSpec((seq, AT_KV_DIM), lambda b, i: (b, 0)),
            pl.BlockSpec((LANES, AT_KV_DIM), lambda b, i: (0, 0)),
            pl.BlockSpec((LANES, AT_KV_DIM), lambda b, i: (0, 0)),
            pl.BlockSpec((1, AT_Q_DIM), lambda b, i: (0, 0)),
        ],
        out_specs=pl.BlockSpec((tq, AT_Q_DIM), lambda b, i: (b * nq + i, 0)),
        out_shape=jax.ShapeDtypeStruct((batch * seq, AT_Q_DIM), BF16),
        scratch_shapes=[pltpu.VMEM((tq, AT_Q_DIM), F32)],
        compiler_params=_cparams("parallel", "arbitrary"),
        name="flash",
    )(q, k, v, k_meta, v_meta, out_norm.reshape(1, AT_Q_DIM))


def _top_desc(x, count):
    rows = x.shape[0]
    ridx = lax.broadcasted_iota(jnp.int32, x.shape, 0).astype(F32)
    out = []
    for _ in range(count):
        m = jnp.max(x, axis=0, keepdims=True)
        out.append(m)
        first = jnp.min(jnp.where(x == m, ridx, float(rows)), axis=0, keepdims=True)
        x = jnp.where(ridx == first, -jnp.inf, x)
    return out


def _peer_route_kernel(h_ref, wq_ref, keys_ref, q_ref, s2_ref, a_ref, b_ref):
    q_t = lax.dot_general(wq_ref[...], h_ref[...], NT_DIMS, preferred_element_type=F32)
    half = PEER_KEY_DIM // 2
    s1 = jnp.dot(keys_ref[0], q_t[:half].astype(BF16), preferred_element_type=F32)
    s2 = jnp.dot(keys_ref[1], q_t[half:].astype(BF16), preferred_element_type=F32)
    t1 = _top_desc(s1, PEER_TOPK + 1)
    t2 = _top_desc(s2, PEER_TOPK + 1)
    t2cat = jnp.concatenate(t2[:PEER_TOPK], axis=0)
    t1cat = jnp.concatenate(t1[:PEER_TOPK], axis=0)
    oct_ = PEER_TOPK // 2
    fill = jnp.full_like(t1[0], -jnp.inf)
    edge = jnp.concatenate([t1[PEER_TOPK] + t2[0], t1[0] + t2[PEER_TOPK]] + [fill] * 6, axis=0)
    cand = jnp.concatenate(
        [t1[0] + t2cat] + [t1[i] + t2cat[:oct_] for i in range(1, oct_)] + [t1cat[oct_:] + t2[0], edge], axis=0)
    top = _top_desc(cand, PEER_TOPK + 1)
    z = jnp.exp(top[0] - top[0])
    for i in range(1, PEER_TOPK):
        z = z + jnp.exp(top[i] - top[0])
    thr = 0.5 * (top[PEER_TOPK - 1] + top[PEER_TOPK])
    q_ref[...] = thr - s1
    s2_ref[...] = s2
    a_ref[...] = jnp.exp(s1 - t1[0]) * (0.5 / z)
    b_ref[...] = jnp.exp(s2 - t2[0])


def _peer_route(hn, wq_t, keys, tm):
    m, d = hn.shape
    spec = pl.BlockSpec((None, N_KEYS, tm), lambda i, h: (h, 0, i))
    shape = jax.ShapeDtypeStruct((PEER_HEADS, N_KEYS, m), F32)
    return pl.pallas_call(
        _peer_route_kernel,
        grid=(m // tm, PEER_HEADS),
        in_specs=[
            pl.BlockSpec((tm, d), lambda i, h: (i, 0)),
            pl.BlockSpec((PEER_KEY_DIM, d), lambda i, h: (h, 0)),
            pl.BlockSpec((None, 2, N_KEYS, PEER_KEY_DIM // 2), lambda i, h: (h, 0, 0, 0)),
        ],
        out_specs=[spec, spec, spec, spec],
        out_shape=[shape, shape, shape, shape],
        compiler_params=_cparams("parallel", "arbitrary"),
        name="peer_route",
    )(hn, wq_t, keys)


def _peer_act_kernel(hn_ref, u_ref, q_ref, s2_ref, a_ref, b_ref, act_ref, *, te):
    j = pl.program_id(1)
    s_t = lax.dot_general(u_ref[...], hn_ref[...], NT_DIMS, preferred_element_type=F32)
    per = te // N_KEYS
    for ii in range(per):
        i = j * per + ii
        gate = None
        for h in range(PEER_HEADS):
            term = jnp.where(s2_ref[h] >= q_ref[h, pl.ds(i, 1), :], b_ref[h], 0.0) * a_ref[h, pl.ds(i, 1), :]
            gate = term if gate is None else gate + term
        x = s_t[ii * N_KEYS:(ii + 1) * N_KEYS, :]
        act = x * (1.0 + lax.erf(x * (2.0 ** -0.5))) * gate
        act_ref[ii * N_KEYS:(ii + 1) * N_KEYS, :] = act.astype(act_ref.dtype)


def _peer_act(hn, u, q, s2, a, b, tm, te):
    m, d = hn.shape
    n_exp = u.shape[0]
    once = pl.Buffered(1)
    tok = pl.BlockSpec((PEER_HEADS, N_KEYS, tm), lambda i, j: (0, 0, i), pipeline_mode=once)
    return pl.pallas_call(
        functools.partial(_peer_act_kernel, te=te),
        grid=(m // tm, n_exp // te),
        in_specs=[
            pl.BlockSpec((tm, d), lambda i, j: (i, 0), pipeline_mode=once),
            pl.BlockSpec((te, d), lambda i, j: (j, 0)),
            tok, tok, tok, tok,
        ],
        out_specs=pl.BlockSpec((te, tm), lambda i, j: (j, i)),
        out_shape=jax.ShapeDtypeStruct((n_exp, m), BF16),
        compiler_params=_cparams("parallel", "arbitrary"),
        name="peer_act",
    )(hn, u, q, s2, a, b)


def _peer_out_kernel(act_ref, v_ref, res_ref, o_ref):
    k = pl.program_id(1)

    @pl.when(k == 0)
    def _():
        o_ref[...] = res_ref[...]

    o_ref[...] += lax.dot_general(act_ref[...], v_ref[...], TN_DIMS, preferred_element_type=F32)


def _peer_out(act_t, v, res, tm, tk):
    n_exp, m = act_t.shape
    d = v.shape[1]
    once = pl.Buffered(1)
    return pl.pallas_call(
        _peer_out_kernel,
        grid=(m // tm, n_exp // tk),
        in_specs=[
            pl.BlockSpec((tk, tm), lambda i, k: (k, i)),
            pl.BlockSpec((tk, d), lambda i, k: (k, 0)),
            pl.BlockSpec((tm, d), lambda i, k: (i, 0), pipeline_mode=once),
        ],
        out_specs=pl.BlockSpec((tm, d), lambda i, k: (i, 0), pipeline_mode=once),
        out_shape=jax.ShapeDtypeStruct((m, d), F32),
        compiler_params=_cparams("parallel", "arbitrary"),
        name="peer_out",
    )(act_t, v, res)


def _tile(m, pref):
    return pref if m % pref == 0 else m


def _encode(x, meta, wts):
    batch, seq, d = x.shape
    m = batch * seq
    xf = x.reshape(m, d)

    xn = _rmsnorm(xf, wts["norm1"], _tile(m, 256))
    proj = _matmul(xn, wts["w_big"], _tile(m, 1024), 512)
    ab = _matmul(xn, wts["w_small"], _tile(m, 1024), LANES)
    proj_meta, ab_meta = meta["proj"], meta["ab"]

    q, k, v = _dn_prep(proj, proj_meta, wts["conv_w"], batch, seq)
    t = seq + CHUNK
    ab_seq = jnp.concatenate([
        jnp.zeros((batch, META_PAD, 4 * DN_HEADS), F32),
        jnp.broadcast_to(ab_meta[None, :, :4 * DN_HEADS], (batch, N_META, 4 * DN_HEADS)),
        ab[:, :4 * DN_HEADS].reshape(batch, seq, 4 * DN_HEADS)], axis=1)
    ab_seq = ab_seq.reshape(batch, t, 2, 2, DN_HEADS)
    ab_col = jnp.transpose(ab_seq, (0, 3, 1, 2, 4)).reshape(batch, 2, t, 2 * DN_HEADS)
    ab_row = jnp.swapaxes(ab_col.reshape(batch, 2, t // CHUNK, CHUNK, 2 * DN_HEADS), -1, -2)
    o_dirs = _dn_scan(q, k, v, ab_col, ab_row, wts["dn_par_col"], wts["dn_par_row"], batch, seq)
    o_dn = _dn_post(o_dirs, proj, wts["dn_out_norm"], batch, seq, _tile(seq, 256))

    cos, sin = _rope_tables(seq)
    tp = _tile(seq, 256)
    aq, ak, av = _attn_prep(proj, cos, sin, wts["q_norm"], wts["k_norm"], tp)
    o_at = _flash(aq, ak, av, meta["k"], meta["v"], wts["attn_out_norm"], batch, seq, _tile(seq, 256), _tile(seq, 1024))

    h2 = _outproj(o_dn, o_at, wts["w_out"], xf, _tile(m, 512), 1024)
    h2n = _rmsnorm(h2, wts["norm2"], _tile(m, 256))
    q, s2, a, b = _peer_route(h2n, wts["peer_wq_t"], wts["peer_keys"], _tile(m, 256))
    act_t = _peer_act(h2n, wts["peer_u"], q, s2, a, b, _tile(m, 512), 1024)
    y = _peer_out(act_t, wts["peer_v"], h2, _tile(m, 512), 1024)
    return y.reshape(batch, seq, d)


def kernel(x_prompt, x_sample, meta_tokens, norm1, w_in, conv_w, a_log, dt_bias, dn_out_norm, q_norm, k_norm,
           attn_out_norm, w_out, norm2, peer_wq, peer_keys, peer_u, peer_v):
    w = w_in[0]
    offs = [0]
    for n in IN_SPLITS:
        offs.append(offs[-1] + n)
    part = [w[:, offs[i]:offs[i + 1]] for i in range(len(IN_SPLITS))]
    w_big = jnp.concatenate([part[0], part[1], part[2], part[5], part[6], part[7], part[8]], axis=1).astype(BF16)
    w_small = jnp.concatenate(
        [part[3], part[4], jnp.zeros((w.shape[0], LANES - 4 * DN_HEADS), w.dtype)], axis=1).astype(BF16)
    cw = conv_w[0]
    wts = {
        "norm1": norm1[0],
        "w_big": w_big,
        "w_small": w_small,
        "conv_w": cw,
        "dn_par_col": jnp.stack([a_log[0], dt_bias[0]], axis=1),
        "dn_par_row": jnp.stack([a_log[0], dt_bias[0]], axis=2),
        "dn_out_norm": dn_out_norm[0],
        "q_norm": q_norm[0],
        "k_norm": k_norm[0],
        "attn_out_norm": attn_out_norm[0],
        "w_out": w_out[0].astype(BF16),
        "norm2": norm2[0],
        "peer_wq_t": peer_wq[0].T.astype(BF16),
        "peer_keys": peer_keys[0].astype(BF16),
        "peer_u": peer_u[0].astype(BF16),
        "peer_v": peer_v[0].astype(BF16),
    }

    mn = _rmsnorm(meta_tokens, wts["norm1"], N_META)
    proj_meta = _matmul(mn, w_big, N_META, 512)
    ab_meta = _matmul(mn, w_small, N_META, LANES)
    ones = jnp.ones((N_META, HEAD_DIM), F32)
    _, k_meta, v_meta = _attn_prep(proj_meta, ones, jnp.zeros_like(ones), wts["q_norm"], wts["k_norm"], N_META)
    pad = ((0, LANES - N_META), (0, 0))
    meta = {"proj": proj_meta, "ab": ab_meta, "k": jnp.pad(k_meta, pad), "v": jnp.pad(v_meta, pad)}

    return (_encode(x_prompt, meta, wts), _encode(x_sample, meta, wts))
```

```python
import functools

import jax
import jax.numpy as jnp
from jax import lax
from jax.experimental import pallas as pl
from jax.experimental.pallas import tpu as pltpu

F32 = jnp.float32
BF16 = jnp.bfloat16

HEAD_DIM = 128
DN_HEADS = 16
DN_DIM = DN_HEADS * HEAD_DIM
AT_HEADS = 16
AT_KV_HEADS = 4
AT_GROUP = AT_HEADS // AT_KV_HEADS
AT_Q_DIM = AT_HEADS * HEAD_DIM
AT_KV_DIM = AT_KV_HEADS * HEAD_DIM
CONV_K = 5
CHUNK = 64
GRID_W = 64
AXIS_DIM = HEAD_DIM // 2
ROPE_THETA = 10000.0
N_META = 16
PEER_HEADS = 8
N_KEYS = 128
PEER_KEY_DIM = 256
PEER_TOPK = 16
EPS = 1e-6
IN_SPLITS = (DN_DIM, DN_DIM, DN_DIM, 2 * DN_HEADS, 2 * DN_HEADS, DN_DIM, AT_Q_DIM, AT_KV_DIM, AT_KV_DIM)

COL_DQ, COL_DK, COL_DV, COL_GATE, COL_AQ = 0, DN_DIM, 2 * DN_DIM, 3 * DN_DIM, 4 * DN_DIM
COL_AK = COL_AQ + AT_Q_DIM
COL_AV = COL_AK + AT_KV_DIM
BIG_DIM = COL_AV + AT_KV_DIM

LANES = 128
META_PAD = CHUNK - N_META
VMEM_LIMIT = 56 * 1024 * 1024

NT_DIMS = (((1,), (1,)), ((), ()))
TN_DIMS = (((0,), (0,)), ((), ()))


def _cparams(*sem):
    return pltpu.CompilerParams(dimension_semantics=sem, vmem_limit_bytes=VMEM_LIMIT)


def _rmsnorm_kernel(x_ref, w_ref, o_ref):
    x = x_ref[...]
    ms = jnp.mean(x * x, axis=-1, keepdims=True)
    o_ref[...] = (x * lax.rsqrt(ms + EPS) * w_ref[...]).astype(o_ref.dtype)


def _rmsnorm(x, w, tm):
    m, d = x.shape
    return pl.pallas_call(
        _rmsnorm_kernel,
        grid=(m // tm,),
        in_specs=[pl.BlockSpec((tm, d), lambda i: (i, 0)), pl.BlockSpec((1, d), lambda i: (0, 0))],
        out_specs=pl.BlockSpec((tm, d), lambda i: (i, 0)),
        out_shape=jax.ShapeDtypeStruct((m, d), BF16),
        compiler_params=_cparams("parallel"),
        name="rmsnorm",
    )(x, w.reshape(1, d))


def _mm_kernel(x_ref, w_ref, o_ref):
    o_ref[...] = jnp.dot(x_ref[...], w_ref[...], preferred_element_type=F32).astype(o_ref.dtype)


def _matmul(x, w, tm, tn, out_dtype=F32):
    m, k = x.shape
    n = w.shape[1]
    return pl.pallas_call(
        _mm_kernel,
        grid=(m // tm, n // tn),
        in_specs=[pl.BlockSpec((tm, k), lambda i, j: (i, 0)), pl.BlockSpec((k, tn), lambda i, j: (0, j))],
        out_specs=pl.BlockSpec((tm, tn), lambda i, j: (i, j)),
        out_shape=jax.ShapeDtypeStruct((m, n), out_dtype),
        compiler_params=_cparams("parallel", "arbitrary"),
        name="matmul",
    )(x, w)


def _outproj_kernel(a_ref, b_ref, wa_ref, wb_ref, x_ref, o_ref):
    acc = jnp.dot(a_ref[...], wa_ref[...], preferred_element_type=F32)
    acc += jnp.dot(b_ref[...], wb_ref[...], preferred_element_type=F32)
    o_ref[...] = x_ref[...] + acc


def _outproj(o_dn, o_at, w_out, x, tm, tn):
    m, k = o_dn.shape
    n = w_out.shape[1]
    return pl.pallas_call(
        _outproj_kernel,
        grid=(m // tm, n // tn),
        in_specs=[
            pl.BlockSpec((tm, k), lambda i, j: (i, 0)),
            pl.BlockSpec((tm, k), lambda i, j: (i, 0)),
            pl.BlockSpec((k, tn), lambda i, j: (0, j)),
            pl.BlockSpec((k, tn), lambda i, j: (1, j)),
            pl.BlockSpec((tm, tn), lambda i, j: (i, j)),
        ],
        out_specs=pl.BlockSpec((tm, tn), lambda i, j: (i, j)),
        out_shape=jax.ShapeDtypeStruct((m, n), F32),
        compiler_params=_cparams("parallel", "arbitrary"),
        name="outproj",
    )(o_dn, o_at, w_out, w_out, x)


CONV_LEAD = 8 + META_PAD + N_META
CONV_ROWS = 512


def _dn_prep_kernel(xq, xk, xv, mq, mk, mv, wq, wk, wv, oq, ok, ov, buf, *, seq):
    half = CONV_K // 2

    def run(x_ref, m_ref, w_ref, o_ref, normalise, scale):
        buf[0:CONV_LEAD - N_META, :] = jnp.zeros((CONV_LEAD - N_META, LANES), F32)
        buf[CONV_LEAD - N_META:CONV_LEAD, :] = m_ref[...]
        for c in range(seq // CONV_ROWS):
            buf[CONV_LEAD + c * CONV_ROWS:CONV_LEAD + (c + 1) * CONV_ROWS, :] = x_ref[c * CONV_ROWS:(c + 1) * CONV_ROWS, :]
        buf[CONV_LEAD + seq:CONV_LEAD + seq + 8, :] = jnp.zeros((8, LANES), F32)
        w = w_ref[...]

        def act(t0, n):
            y = w[0:1, :] * buf[t0 + 8 - half:t0 + 8 - half + n, :]
            for k in range(1, CONV_K):
                y = y + w[k:k + 1, :] * buf[t0 + 8 - half + k:t0 + 8 - half + k + n, :]
            y = y * jax.nn.sigmoid(y)
            if normalise:
                y = y * lax.rsqrt(jnp.sum(y * y, axis=-1, keepdims=True) + EPS) * scale
            return y.astype(o_ref.dtype)

        o_ref[0:META_PAD, :] = jnp.zeros((META_PAD, LANES), o_ref.dtype)
        o_ref[META_PAD:CHUNK, :] = act(META_PAD, N_META)
        for c in range(seq // CONV_ROWS):
            o_ref[CHUNK + c * CONV_ROWS:CHUNK + (c + 1) * CONV_ROWS, :] = act(CHUNK + c * CONV_ROWS, CONV_ROWS)

    run(xq, mq, wq, oq, True, HEAD_DIM ** -0.5)
    run(xk, mk, wk, ok, True, 1.0)
    run(xv, mv, wv, ov, False, 1.0)


def _dn_prep(proj, proj_meta, conv_w, batch, seq):
    t = seq + CHUNK
    proj3 = proj.reshape(batch, seq, BIG_DIM)
    hq, hk, hv = COL_DQ // LANES, COL_DK // LANES, COL_DV // LANES

    def xspec(off):
        return pl.BlockSpec((None, seq, LANES), lambda b, h: (b, 0, off + h))

    def mspec(off):
        return pl.BlockSpec((N_META, LANES), lambda b, h: (0, off + h))

    def wspec(off):
        return pl.BlockSpec((CONV_K, LANES), lambda b, h: (0, off + h))

    ospec = pl.BlockSpec((None, t, LANES), lambda b, h: (b, 0, h))
    oshape = jax.ShapeDtypeStruct((batch, t, DN_DIM), BF16)
    return pl.pallas_call(
        functools.partial(_dn_prep_kernel, seq=seq),
        grid=(batch, DN_HEADS),
        in_specs=[xspec(hq), xspec(hk), xspec(hv), mspec(hq), mspec(hk), mspec(hv), wspec(hq), wspec(hk), wspec(hv)],
        out_specs=[ospec, ospec, ospec],
        out_shape=[oshape, oshape, oshape],
        scratch_shapes=[pltpu.VMEM((seq + CONV_LEAD + 8, LANES), F32)],
        compiler_params=_cparams("parallel", "parallel"),
        name="dn_prep",
    )(proj3, proj3, proj3, proj_meta, proj_meta, proj_meta, conv_w, conv_w, conv_w)


def _dn_scan_kernel(qf_ref, kf_ref, vf_ref, qb_ref, kb_ref, vb_ref, abcf_ref, abrf_ref, abcb_ref, abrb_ref,
                    pc_ref, pr_ref, of_ref, ob_ref, state, o_scr):
    n = pl.program_id(1)
    n_chunks = pl.num_programs(1)

    @pl.when(n == 0)
    def _():
        state[...] = jnp.zeros_like(state)

    row = lax.broadcasted_iota(jnp.int32, (CHUNK, CHUNK), 0)
    col = lax.broadcasted_iota(jnp.int32, (CHUNK, CHUNK), 1)
    eye = (row == col).astype(F32)
    levels = []
    for sh in range(6):
        levels.append(jnp.logical_and((row >> (sh + 1)) == (col >> (sh + 1)), (row >> sh) != (col >> sh)))
    t_c = lax.broadcasted_iota(jnp.int32, (CHUNK, 1), 0)
    t_r = lax.broadcasted_iota(jnp.int32, (1, CHUNK), 1)
    hi = lax.Precision.HIGHEST

    def prepare(d, chunk, abc_ref, abr_ref):
        order = (row - col) if d == 0 else (col - row)
        incl = order >= 0
        first = chunk == 0
        live_c = jnp.logical_or(jnp.logical_not(first), t_c >= META_PAD)
        live_r = jnp.logical_or(jnp.logical_not(first), t_r >= META_PAD)
        abc = abc_ref[...]
        abr = abr_ref[...]
        beta_c = jnp.where(live_c, jax.nn.sigmoid(abc[:, 0:DN_HEADS]), 0.0)
        g_c = jnp.where(live_c, -jnp.exp(pc_ref[d, 0:1, :]) * jax.nn.softplus(abc[:, DN_HEADS:] + pc_ref[d, 1:2, :]), 0.0)
        g_r = jnp.where(live_r, -jnp.exp(pr_ref[d, :, 0:1]) * jax.nn.softplus(abr[DN_HEADS:, :] + pr_ref[d, :, 1:2]), 0.0)
        tri = incl.astype(F32)
        gcum_c = jnp.dot(tri, g_c, precision=hi, preferred_element_type=F32)
        gcum_r = lax.dot_general(g_r, tri, NT_DIMS, precision=hi, preferred_element_type=F32)
        gtot = jnp.sum(g_c, axis=0, keepdims=True)
        return dict(incl=incl, strict=order > 0, beta=beta_c, gcum_c=gcum_c, gcum_r=gcum_r,
                    eg=jnp.exp(gcum_c), ekd=jnp.exp(gtot - gcum_c), cdec=jnp.exp(gtot))

    dirs = [prepare(0, n, abcf_ref, abrf_ref), prepare(1, n_chunks - 1 - n, abcb_ref, abrb_ref)]
    qkv = [(qf_ref, kf_ref, vf_ref), (qb_ref, kb_ref, vb_ref)]

    chains = [(d, h) for d in range(2) for h in range(DN_HEADS)]
    idx = range(len(chains))
    sls = [slice(h * HEAD_DIM, (h + 1) * HEAD_DIM) for _, h in chains]
    par = [dirs[d] for d, _ in chains]
    col_of = [slice(h, h + 1) for _, h in chains]

    def dot(x, y):
        return jnp.dot(x, y, preferred_element_type=F32)

    qh = [qkv[d][0][:, sls[c]] for c, (d, _) in enumerate(chains)]
    kh = [qkv[d][1][:, sls[c]] for c, (d, _) in enumerate(chains)]
    bc = [par[c]["beta"][:, col_of[c]] for c in idx]
    egh = [par[c]["eg"][:, col_of[c]] for c in idx]
    kk = [lax.dot_general(kh[c], kh[c], NT_DIMS, preferred_element_type=F32) for c in idx]
    qk = [lax.dot_general(qh[c], kh[c], NT_DIMS, preferred_element_type=F32) for c in idx]
    decay = [jnp.exp(jnp.where(par[c]["incl"], par[c]["gcum_c"][:, col_of[c]] - par[c]["gcum_r"][col_of[c], :], -jnp.inf))
             for c in idx]
    a = [jnp.where(par[c]["strict"], bc[c] * kk[c] * decay[c], 0.0) for c in idx]
    qkd = [(qk[c] * decay[c]).astype(BF16) for c in idx]
    tinv = [eye - jnp.where(levels[0], a[c], 0.0) for c in idx]
    for lvl in levels[1:]:
        cb = [jnp.where(lvl, a[c], 0.0).astype(BF16) for c in idx]
        tb = [tinv[c].astype(BF16) for c in idx]
        tc = [dot(tb[c], cb[c]).astype(BF16) for c in idx]
        tinv = [tinv[c] - dot(tc[c], tb[c]) for c in idx]
    kf = [kh[c].astype(F32) for c in idx]
    rhs = [jnp.concatenate([qkv[d][2][:, sls[c]].astype(F32) * bc[c], kf[c] * (bc[c] * egh[c])], axis=1).astype(BF16)
           for c, (d, _) in enumerate(chains)]
    sol = [dot(tinv[c].astype(BF16), rhs[c]) for c in idx]
    q_dec = [(qh[c].astype(F32) * egh[c]).astype(BF16) for c in idx]
    k_dec = [(kf[c] * par[c]["ekd"][:, col_of[c]]).astype(BF16) for c in idx]
    s_old = [state[c] for c in idx]
    sb = [s_old[c].astype(BF16) for c in idx]
    wb = [sol[c][:, HEAD_DIM:].astype(BF16) for c in idx]
    vb = [(sol[c][:, :HEAD_DIM] - dot(wb[c], sb[c])).astype(BF16) for c in idx]
    o_inter = [dot(q_dec[c], sb[c]) for c in idx]
    for c, (d, _) in enumerate(chains):
        o_scr[d, :, sls[c]] = o_inter[c] + dot(qkd[c], vb[c])
    for c in idx:
        state[c] = s_old[c] * par[c]["cdec"][:, col_of[c]] + lax.dot_general(
            k_dec[c], vb[c], TN_DIMS, preferred_element_type=F32)

    @pl.when(n > 0)
    def _():
        of_ref[...] = o_scr[0]

    @pl.when(n < n_chunks - 1)
    def _():
        ob_ref[...] = o_scr[1]


def _dn_scan(q, k, v, ab_col, ab_row, par_col, par_row, batch, seq):
    n_chunks = (seq + CHUNK) // CHUNK
    last = n_chunks - 1
    fspec = pl.BlockSpec((None, CHUNK, DN_DIM), lambda b, n: (b, n, 0))
    bspec = pl.BlockSpec((None, CHUNK, DN_DIM), lambda b, n: (b, last - n, 0))
    oshape = jax.ShapeDtypeStruct((batch, seq, DN_DIM), F32)
    return pl.pallas_call(
        _dn_scan_kernel,
        grid=(batch, n_chunks),
        in_specs=[
            fspec, fspec, fspec, bspec, bspec, bspec,
            pl.BlockSpec((None, None, CHUNK, 2 * DN_HEADS), lambda b, n: (b, 0, n, 0)),
            pl.BlockSpec((None, None, None, 2 * DN_HEADS, CHUNK), lambda b, n: (b, 0, n, 0, 0)),
            pl.BlockSpec((None, None, CHUNK, 2 * DN_HEADS), lambda b, n: (b, 1, last - n, 0)),
            pl.BlockSpec((None, None, None, 2 * DN_HEADS, CHUNK), lambda b, n: (b, 1, last - n, 0, 0)),
            pl.BlockSpec((2, 2, DN_HEADS), lambda b, n: (0, 0, 0)),
            pl.BlockSpec((2, DN_HEADS, 2), lambda b, n: (0, 0, 0)),
        ],
        out_specs=[
            pl.BlockSpec((None, CHUNK, DN_DIM), lambda b, n: (b, jnp.maximum(n - 1, 0), 0)),
            pl.BlockSpec((None, CHUNK, DN_DIM), lambda b, n: (b, jnp.maximum(last - 1 - n, 0), 0)),
        ],
        out_shape=[oshape, oshape],
        scratch_shapes=[pltpu.VMEM((2 * DN_HEADS, HEAD_DIM, HEAD_DIM), F32), pltpu.VMEM((2, CHUNK, DN_DIM), F32)],
        compiler_params=_cparams("parallel", "arbitrary"),
        name="dn_scan",
    )(q, k, v, q, k, v, ab_col, ab_row, ab_col, ab_row, par_col, par_row)


def _dn_post_kernel(of_ref, ob_ref, g_ref, w_ref, y_ref):
    w = w_ref[...]
    for h in range(DN_HEADS):
        sl = slice(h * HEAD_DIM, (h + 1) * HEAD_DIM)
        o = of_ref[:, sl] + ob_ref[:, sl]
        o = o * lax.rsqrt(jnp.mean(o * o, axis=-1, keepdims=True) + EPS) * w
        g = g_ref[:, sl]
        y_ref[:, sl] = (o * (g * jax.nn.sigmoid(g))).astype(y_ref.dtype)


def _dn_post(o_fwd, o_bwd, proj, out_norm, batch, seq, tm):
    nb = seq // tm
    ospec = pl.BlockSpec((None, tm, DN_DIM), lambda b, i: (b, i, 0))
    return pl.pallas_call(
        _dn_post_kernel,
        grid=(batch, nb),
        in_specs=[
            ospec, ospec,
            pl.BlockSpec((tm, DN_DIM), lambda b, i: (b * nb + i, COL_GATE // DN_DIM)),
            pl.BlockSpec((1, HEAD_DIM), lambda b, i: (0, 0)),
        ],
        out_specs=pl.BlockSpec((tm, DN_DIM), lambda b, i: (b * nb + i, 0)),
        out_shape=jax.ShapeDtypeStruct((batch * seq, DN_DIM), BF16),
        compiler_params=_cparams("parallel", "parallel"),
        name="dn_post",
    )(o_fwd, o_bwd, proj, out_norm.reshape(1, HEAD_DIM))


def _rope_tables(seq):
    rows = seq // GRID_W
    row = jnp.repeat(jnp.arange(rows, dtype=F32), GRID_W)
    col = jnp.tile(jnp.arange(GRID_W, dtype=F32), rows)
    inv_freq = ROPE_THETA ** (-jnp.arange(0, AXIS_DIM, 2, dtype=F32) / AXIS_DIM)
    ar = row[:, None] * inv_freq
    ac = col[:, None] * inv_freq
    ang = jnp.concatenate([ar, ar, ac, ac], -1)
    return jnp.cos(ang), jnp.sin(ang)


def _attn_prep_kernel(q_ref, k_ref, v_ref, cos_ref, sin_ref, qw_ref, kw_ref, qo_ref, ko_ref, vo_ref):
    cos = cos_ref[...]
    sin = sin_ref[...]
    lane = lax.broadcasted_iota(jnp.int32, cos.shape, 1)
    first_half = (lane % (HEAD_DIM // 2)) < (HEAD_DIM // 4)

    def norm_rope(x, w, scale):
        x = x * lax.rsqrt(jnp.mean(x * x, axis=-1, keepdims=True) + EPS) * w
        rot = jnp.where(first_half, -pltpu.roll(x, HEAD_DIM - HEAD_DIM // 4, 1), pltpu.roll(x, HEAD_DIM // 4, 1))
        return (x * cos + rot * sin) * scale

    for h in range(AT_HEADS):
        sl = slice(h * HEAD_DIM, (h + 1) * HEAD_DIM)
        qo_ref[:, sl] = norm_rope(q_ref[:, sl], qw_ref[...], HEAD_DIM ** -0.5).astype(qo_ref.dtype)
    for h in range(AT_KV_HEADS):
        sl = slice(h * HEAD_DIM, (h + 1) * HEAD_DIM)
        ko_ref[:, sl] = norm_rope(k_ref[:, sl], kw_ref[...], 1.0).astype(ko_ref.dtype)
    vo_ref[...] = v_ref[...].astype(vo_ref.dtype)


def _attn_prep(proj, cos, sin, q_norm, k_norm, tm):
    m = proj.shape[0]
    nrep = cos.shape[0] // tm
    return pl.pallas_call(
        _attn_prep_kernel,
        grid=(m // tm,),
        in_specs=[
            pl.BlockSpec((tm, AT_Q_DIM), lambda i: (i, COL_AQ // AT_Q_DIM)),
            pl.BlockSpec((tm, AT_KV_DIM), lambda i: (i, COL_AK // AT_KV_DIM)),
            pl.BlockSpec((tm, AT_KV_DIM), lambda i: (i, COL_AV // AT_KV_DIM)),
            pl.BlockSpec((tm, HEAD_DIM), lambda i: (i % nrep, 0)),
            pl.BlockSpec((tm, HEAD_DIM), lambda i: (i % nrep, 0)),
            pl.BlockSpec((1, HEAD_DIM), lambda i: (0, 0)),
            pl.BlockSpec((1, HEAD_DIM), lambda i: (0, 0)),
        ],
        out_specs=[
            pl.BlockSpec((tm, AT_Q_DIM), lambda i: (i, 0)),
            pl.BlockSpec((tm, AT_KV_DIM), lambda i: (i, 0)),
            pl.BlockSpec((tm, AT_KV_DIM), lambda i: (i, 0)),
        ],
        out_shape=[
            jax.ShapeDtypeStruct((m, AT_Q_DIM), BF16),
            jax.ShapeDtypeStruct((m, AT_KV_DIM), BF16),
            jax.ShapeDtypeStruct((m, AT_KV_DIM), BF16),
        ],
        compiler_params=_cparams("parallel"),
        name="attn_prep",
    )(proj, proj, proj, cos, sin, q_norm.reshape(1, HEAD_DIM), k_norm.reshape(1, HEAD_DIM))


def _flash_kernel(q_ref, k_ref, v_ref, km_ref, vm_ref, w_ref, o_ref, o_scr, *, seq, tq, tk):
    lane = lax.broadcasted_iota(jnp.int32, (1, LANES), 1)
    is_meta = lane < N_META
    n_kv = seq // tk
    for g in range(AT_KV_HEADS):
        ks = slice(g * HEAD_DIM, (g + 1) * HEAD_DIM)
        qg = jnp.concatenate(
            [q_ref[:, (AT_GROUP * g + j) * HEAD_DIM:(AT_GROUP * g + j + 1) * HEAD_DIM] for j in range(AT_GROUP)], axis=0)
        s0 = lax.dot_general(qg, km_ref[:, ks], NT_DIMS, preferred_element_type=F32)
        s0 = jnp.where(is_meta, s0, -jnp.inf)
        m0 = jnp.max(s0, axis=-1, keepdims=True)
        p0 = jnp.exp(s0 - m0)
        l0 = jnp.sum(p0, axis=-1, keepdims=True)
        acc0 = jnp.dot(p0.astype(BF16), vm_ref[:, ks], preferred_element_type=F32)

        def body(j, carry):
            m, l, acc = carry
            start = pl.multiple_of(j * tk, tk)
            kb = k_ref[pl.ds(start, tk), ks]
            vb = v_ref[pl.ds(start, tk), ks]
            s = lax.dot_general(qg, kb, NT_DIMS, preferred_element_type=F32)
            m_new = jnp.maximum(m, jnp.max(s, axis=-1, keepdims=True))
            alpha = jnp.exp(m - m_new)
            p = jnp.exp(s - m_new)
            l = alpha * l + jnp.sum(p, axis=-1, keepdims=True)
            acc = alpha * acc + jnp.dot(p.astype(BF16), vb, preferred_element_type=F32)
            return m_new, l, acc

        unroll = max(u for u in (1, 2, 4) if n_kv % u == 0)
        _, l, acc = lax.fori_loop(0, n_kv, body, (m0, l0, acc0), unroll=unroll)
        og = acc / l
        for j in range(AT_GROUP):
            h = AT_GROUP * g + j
            o_scr[:, h * HEAD_DIM:(h + 1) * HEAD_DIM] = og[j * tq:(j + 1) * tq, :]
    o = o_scr[...]
    o_ref[...] = (o * lax.rsqrt(jnp.mean(o * o, axis=-1, keepdims=True) + EPS) * w_ref[...]).astype(o_ref.dtype)


def _flash(q, k, v, k_meta, v_meta, out_norm, batch, seq, tq, tk):
    nq = seq // tq
    return pl.pallas_call(
        functools.partial(_flash_kernel, seq=seq, tq=tq, tk=tk),
        grid=(batch, nq),
        in_specs=[
            pl.BlockSpec((tq, AT_Q_DIM), lambda b, i: (b * nq + i, 0)),
            pl.BlockSpec((seq, AT_KV_DIM), lambda b, i: (b, 0)),
            pl.BlockSpec((seq, AT_KV_DIM), lambda b, i: (b, 0)),
            pl.BlockSpec((LANES, AT_KV_DIM), lambda b, i: (0, 0)),
            pl.BlockSpec((LANES, AT_KV_DIM), lambda b, i: (0, 0)),
            pl.BlockSpec((1, AT_Q_DIM), lambda b, i: (0, 0)),
        ],
        out_specs=pl.BlockSpec((tq, AT_Q_DIM), lambda b, i: (b * nq + i, 0)),
        out_shape=jax.ShapeDtypeStruct((batch * seq, AT_Q_DIM), BF16),
        scratch_shapes=[pltpu.VMEM((tq, AT_Q_DIM), F32)],
        compiler_params=_cparams("parallel", "arbitrary"),
        name="flash",
    )(q, k, v, k_meta, v_meta, out_norm.reshape(1, AT_Q_DIM))


def _sort_network(n):
    pairs = []

    def merge(lo, cnt, r):
        step = 2 * r
        if step < cnt:
            merge(lo, cnt, step)
            merge(lo + r, cnt, step)
            pairs.extend((i, i + r) for i in range(lo + r, lo + cnt - r, step))
        else:
            pairs.append((lo, lo + r))

    def sort(lo, cnt):
        if cnt > 1:
            sort(lo, cnt // 2)
            sort(lo + cnt // 2, cnt // 2)
            merge(lo, cnt, 1)

    sort(0, n)
    return pairs


SUBLANES = 8


def _top_desc_tiles(x, count):
    n = x.shape[0] // SUBLANES
    tiles = [x[SUBLANES * v:SUBLANES * (v + 1), :] for v in range(n)]
    for i, j in _sort_network(n):
        tiles[i], tiles[j] = jnp.maximum(tiles[i], tiles[j]), jnp.minimum(tiles[i], tiles[j])
    sub = lax.broadcasted_iota(jnp.int32, tiles[0].shape, 0).astype(F32)
    out = []
    for k in range(count):
        head = tiles[0]
        m = jnp.max(head, axis=0, keepdims=True)
        out.append(m)
        if k == count - 1:
            break
        first = jnp.min(jnp.where(head == m, sub, float(SUBLANES)), axis=0, keepdims=True)
        pop = sub == first
        for v in range(min(n, count - k - 1)):
            below = tiles[v + 1] if v + 1 < n else jnp.full_like(head, -jnp.inf)
            tiles[v] = jnp.where(pop, below, tiles[v])
    return out


def _peer_route_kernel(h_ref, wq_ref, keys_ref, q_ref, s2_ref, a_ref, b_ref):
    q_t = lax.dot_general(wq_ref[...], h_ref[...], NT_DIMS, preferred_element_type=F32)
    half = PEER_KEY_DIM // 2
    s1 = jnp.dot(keys_ref[0], q_t[:half].astype(BF16), preferred_element_type=F32)
    s2 = jnp.dot(keys_ref[1], q_t[half:].astype(BF16), preferred_element_type=F32)
    t1 = _top_desc_tiles(s1, PEER_TOPK + 1)
    t2 = _top_desc_tiles(s2, PEER_TOPK + 1)
    t2cat = jnp.concatenate(t2[:PEER_TOPK], axis=0)
    t1cat = jnp.concatenate(t1[:PEER_TOPK], axis=0)
    oct_ = PEER_TOPK // 2
    fill = jnp.full_like(t1[0], -jnp.inf)
    edge = jnp.concatenate([t1[PEER_TOPK] + t2[0], t1[0] + t2[PEER_TOPK]] + [fill] * 6, axis=0)
    groups = [t1[0] + t2cat] + [t1[i] + t2cat[:oct_] for i in range(1, oct_)] + [t1cat[oct_:] + t2[0], edge]
    rows = sum(g.shape[0] for g in groups)
    pad_rows = pl.next_power_of_2(rows // SUBLANES) * SUBLANES - rows
    cand = jnp.concatenate(groups + [jnp.full((pad_rows, fill.shape[1]), -jnp.inf, F32)], axis=0)
    top = _top_desc_tiles(cand, PEER_TOPK + 1)
    z = jnp.exp(top[0] - top[0])
    for i in range(1, PEER_TOPK):
        z = z + jnp.exp(top[i] - top[0])
    thr = 0.5 * (top[PEER_TOPK - 1] + top[PEER_TOPK])
    q_ref[...] = thr - s1
    s2_ref[...] = s2
    a_ref[...] = jnp.exp(s1 - t1[0]) * (0.5 / z)
    b_ref[...] = jnp.exp(s2 - t2[0])


def _peer_route(hn, wq_t, keys, tm):
    m, d = hn.shape
    spec = pl.BlockSpec((None, N_KEYS, tm), lambda i, h: (h, 0, i))
    shape = jax.ShapeDtypeStruct((PEER_HEADS, N_KEYS, m), F32)
    return pl.pallas_call(
        _peer_route_kernel,
        grid=(m // tm, PEER_HEADS),
        in_specs=[
            pl.BlockSpec((tm, d), lambda i, h: (i, 0)),
            pl.BlockSpec((PEER_KEY_DIM, d), lambda i, h: (h, 0)),
            pl.BlockSpec((None, 2, N_KEYS, PEER_KEY_DIM // 2), lambda i, h: (h, 0, 0, 0)),
        ],
        out_specs=[spec, spec, spec, spec],
        out_shape=[shape, shape, shape, shape],
        compiler_params=_cparams("parallel", "arbitrary"),
        name="peer_route",
    )(hn, wq_t, keys)


def _peer_act_kernel(hn_ref, u_ref, q_ref, s2_ref, a_ref, b_ref, act_ref, *, te):
    j = pl.program_id(1)
    s_t = lax.dot_general(u_ref[...], hn_ref[...], NT_DIMS, preferred_element_type=F32)
    per = te // N_KEYS
    for ii in range(per):
        i = j * per + ii
        gate = None
        for h in range(PEER_HEADS):
            term = jnp.where(s2_ref[h] >= q_ref[h, pl.ds(i, 1), :], b_ref[h], 0.0) * a_ref[h, pl.ds(i, 1), :]
            gate = term if gate is None else gate + term
        x = s_t[ii * N_KEYS:(ii + 1) * N_KEYS, :]
        act = x * (1.0 + lax.erf(x * (2.0 ** -0.5))) * gate
        act_ref[ii * N_KEYS:(ii + 1) * N_KEYS, :] = act.astype(act_ref.dtype)


def _peer_act(hn, u, q, s2, a, b, tm, te):
    m, d = hn.shape
    n_exp = u.shape[0]
    once = pl.Buffered(1)
    tok = pl.BlockSpec((PEER_HEADS, N_KEYS, tm), lambda i, j: (0, 0, i), pipeline_mode=once)
    return pl.pallas_call(
        functools.partial(_peer_act_kernel, te=te),
        grid=(m // tm, n_exp // te),
        in_specs=[
            pl.BlockSpec((tm, d), lambda i, j: (i, 0), pipeline_mode=once),
            pl.BlockSpec((te, d), lambda i, j: (j, 0)),
            tok, tok, tok, tok,
        ],
        out_specs=pl.BlockSpec((te, tm), lambda i, j: (j, i)),
        out_shape=jax.ShapeDtypeStruct((n_exp, m), BF16),
        compiler_params=_cparams("parallel", "arbitrary"),
        name="peer_act",
    )(hn, u, q, s2, a, b)


def _peer_out_kernel(act_ref, v_ref, res_ref, o_ref):
    k = pl.program_id(1)

    @pl.when(k == 0)
    def _():
        o_ref[...] = res_ref[...]

    o_ref[...] += lax.dot_general(act_ref[...], v_ref[...], TN_DIMS, preferred_element_type=F32)


def _peer_out(act_t, v, res, tm, tk):
    n_exp, m = act_t.shape
    d = v.shape[1]
    once = pl.Buffered(1)
    return pl.pallas_call(
        _peer_out_kernel,
        grid=(m // tm, n_exp // tk),
        in_specs=[
            pl.BlockSpec((tk, tm), lambda i, k: (k, i)),
            pl.BlockSpec((tk, d), lambda i, k: (k, 0)),
            pl.BlockSpec((tm, d), lambda i, k: (i, 0), pipeline_mode=once),
        ],
        out_specs=pl.BlockSpec((tm, d), lambda i, k: (i, 0), pipeline_mode=once),
        out_shape=jax.ShapeDtypeStruct((m, d), F32),
        compiler_params=_cparams("parallel", "arbitrary"),
        name="peer_out",
    )(act_t, v, res)


def _tile(m, pref):
    return pref if m % pref == 0 else m


def _encode(x, meta, wts):
    batch, seq, d = x.shape
    m = batch * seq
    xf = x.reshape(m, d)

    xn = _rmsnorm(xf, wts["norm1"], _tile(m, 256))
    proj = _matmul(xn, wts["w_big"], _tile(m, 1024), 512)
    ab = _matmul(xn, wts["w_small"], _tile(m, 1024), LANES)
    proj_meta, ab_meta = meta["proj"], meta["ab"]

    q, k, v = _dn_prep(proj, proj_meta, wts["conv_w"], batch, seq)
    t = seq + CHUNK
    ab_seq = jnp.concatenate([
        jnp.zeros((batch, META_PAD, 4 * DN_HEADS), F32),
        jnp.broadcast_to(ab_meta[None, :, :4 * DN_HEADS], (batch, N_META, 4 * DN_HEADS)),
        ab[:, :4 * DN_HEADS].reshape(batch, seq, 4 * DN_HEADS)], axis=1)
    ab_seq = ab_seq.reshape(batch, t, 2, 2, DN_HEADS)
    ab_col = jnp.transpose(ab_seq, (0, 3, 1, 2, 4)).reshape(batch, 2, t, 2 * DN_HEADS)
    ab_row = jnp.swapaxes(ab_col.reshape(batch, 2, t // CHUNK, CHUNK, 2 * DN_HEADS), -1, -2)
    o_fwd, o_bwd = _dn_scan(q, k, v, ab_col, ab_row, wts["dn_par_col"], wts["dn_par_row"], batch, seq)
    o_dn = _dn_post(o_fwd, o_bwd, proj, wts["dn_out_norm"], batch, seq, _tile(seq, 256))

    cos, sin = _rope_tables(seq)
    tp = _tile(seq, 256)
    aq, ak, av = _attn_prep(proj, cos, sin, wts["q_norm"], wts["k_norm"], tp)
    o_at = _flash(aq, ak, av, meta["k"], meta["v"], wts["attn_out_norm"], batch, seq, _tile(seq, 256), _tile(seq, 1024))

    h2 = _outproj(o_dn, o_at, wts["w_out"], xf, _tile(m, 512), 1024)
    h2n = _rmsnorm(h2, wts["norm2"], _tile(m, 256))
    q, s2, a, b = _peer_route(h2n, wts["peer_wq_t"], wts["peer_keys"], _tile(m, 512))
    act_t = _peer_act(h2n, wts["peer_u"], q, s2, a, b, _tile(m, 512), 1024)
    y = _peer_out(act_t, wts["peer_v"], h2, _tile(m, 512), 1024)
    return y.reshape(batch, seq, d)


def kernel(x_prompt, x_sample, meta_tokens, norm1, w_in, conv_w, a_log, dt_bias, dn_out_norm, q_norm, k_norm,
           attn_out_norm, w_out, norm2, peer_wq, peer_keys, peer_u, peer_v):
    w = w_in[0]
    offs = [0]
    for n in IN_SPLITS:
        offs.append(offs[-1] + n)
    part = [w[:, offs[i]:offs[i + 1]] for i in range(len(IN_SPLITS))]
    w_big = jnp.concatenate([part[0], part[1], part[2], part[5], part[6], part[7], part[8]], axis=1).astype(BF16)
    w_small = jnp.concatenate(
        [part[3], part[4], jnp.zeros((w.shape[0], LANES - 4 * DN_HEADS), w.dtype)], axis=1).astype(BF16)
    cw = conv_w[0]
    wts = {
        "norm1": norm1[0],
        "w_big": w_big,
        "w_small": w_small,
        "conv_w": cw,
        "dn_par_col": jnp.stack([a_log[0], dt_bias[0]], axis=1),
        "dn_par_row": jnp.stack([a_log[0], dt_bias[0]], axis=2),
        "dn_out_norm": dn_out_norm[0],
        "q_norm": q_norm[0],
        "k_norm": k_norm[0],
        "attn_out_norm": attn_out_norm[0],
        "w_out": w_out[0].astype(BF16),
        "norm2": norm2[0],
        "peer_wq_t": peer_wq[0].T.astype(BF16),
        "peer_keys": peer_keys[0].astype(BF16),
        "peer_u": peer_u[0].astype(BF16),
        "peer_v": peer_v[0].astype(BF16),
    }

    mn = _rmsnorm(meta_tokens, wts["norm1"], N_META)
    proj_meta = _matmul(mn, w_big, N_META, 512)
    ab_meta = _matmul(mn, w_small, N_META, LANES)
    ones = jnp.ones((N_META, HEAD_DIM), F32)
    _, k_meta, v_meta = _attn_prep(proj_meta, ones, jnp.zeros_like(ones), wts["q_norm"], wts["k_norm"], N_META)
    pad = ((0, LANES - N_META), (0, 0))
    meta = {"proj": proj_meta, "ab": ab_meta, "k": jnp.pad(k_meta, pad), "v": jnp.pad(v_meta, pad)}

    return (_encode(x_prompt, meta, wts), _encode(x_sample, meta, wts))
```

```python
import functools

import jax
import jax.numpy as jnp
from jax import lax
from jax.experimental import pallas as pl
from jax.experimental.pallas import tpu as pltpu

F32 = jnp.float32
BF16 = jnp.bfloat16

HEAD_DIM = 128
DN_HEADS = 16
DN_DIM = DN_HEADS * HEAD_DIM
AT_HEADS = 16
AT_KV_HEADS = 4
AT_GROUP = AT_HEADS // AT_KV_HEADS
AT_Q_DIM = AT_HEADS * HEAD_DIM
AT_KV_DIM = AT_KV_HEADS * HEAD_DIM
CONV_K = 5
CHUNK = 64
GRID_W = 64
AXIS_DIM = HEAD_DIM // 2
ROPE_THETA = 10000.0
N_META = 16
PEER_HEADS = 8
N_KEYS = 128
PEER_KEY_DIM = 256
PEER_TOPK = 16
EPS = 1e-6
IN_SPLITS = (DN_DIM, DN_DIM, DN_DIM, 2 * DN_HEADS, 2 * DN_HEADS, DN_DIM, AT_Q_DIM, AT_KV_DIM, AT_KV_DIM)

COL_DQ, COL_DK, COL_DV, COL_GATE, COL_AQ = 0, DN_DIM, 2 * DN_DIM, 3 * DN_DIM, 4 * DN_DIM
COL_AK = COL_AQ + AT_Q_DIM
COL_AV = COL_AK + AT_KV_DIM
BIG_DIM = COL_AV + AT_KV_DIM

LANES = 128
META_PAD = CHUNK - N_META
VMEM_LIMIT = 56 * 1024 * 1024

NT_DIMS = (((1,), (1,)), ((), ()))
TN_DIMS = (((0,), (0,)), ((), ()))


def _cparams(*sem):
    return pltpu.CompilerParams(dimension_semantics=sem, vmem_limit_bytes=VMEM_LIMIT)


def _rmsnorm_kernel(x_ref, w_ref, o_ref):
    x = x_ref[...]
    ms = jnp.mean(x * x, axis=-1, keepdims=True)
    o_ref[...] = (x * lax.rsqrt(ms + EPS) * w_ref[...]).astype(o_ref.dtype)


def _rmsnorm(x, w, tm):
    m, d = x.shape
    return pl.pallas_call(
        _rmsnorm_kernel,
        grid=(m // tm,),
        in_specs=[pl.BlockSpec((tm, d), lambda i: (i, 0)), pl.BlockSpec((1, d), lambda i: (0, 0))],
        out_specs=pl.BlockSpec((tm, d), lambda i: (i, 0)),
        out_shape=jax.ShapeDtypeStruct((m, d), BF16),
        compiler_params=_cparams("parallel"),
        name="rmsnorm",
    )(x, w.reshape(1, d))


def _mm_kernel(x_ref, w_ref, o_ref):
    o_ref[...] = jnp.dot(x_ref[...], w_ref[...], preferred_element_type=F32).astype(o_ref.dtype)


def _matmul(x, w, tm, tn, out_dtype=F32):
    m, k = x.shape
    n = w.shape[1]
    return pl.pallas_call(
        _mm_kernel,
        grid=(m // tm, n // tn),
        in_specs=[pl.BlockSpec((tm, k), lambda i, j: (i, 0)), pl.BlockSpec((k, tn), lambda i, j: (0, j))],
        out_specs=pl.BlockSpec((tm, tn), lambda i, j: (i, j)),
        out_shape=jax.ShapeDtypeStruct((m, n), out_dtype),
        compiler_params=_cparams("parallel", "arbitrary"),
        name="matmul",
    )(x, w)


def _outproj_kernel(a_ref, b_ref, wa_ref, wb_ref, x_ref, o_ref):
    acc = jnp.dot(a_ref[...], wa_ref[...], preferred_element_type=F32)
    acc += jnp.dot(b_ref[...], wb_ref[...], preferred_element_type=F32)
    o_ref[...] = x_ref[...] + acc


def _outproj(o_dn, o_at, w_out, x, tm, tn):
    m, k = o_dn.shape
    n = w_out.shape[1]
    return pl.pallas_call(
        _outproj_kernel,
        grid=(m // tm, n // tn),
        in_specs=[
            pl.BlockSpec((tm, k), lambda i, j: (i, 0)),
            pl.BlockSpec((tm, k), lambda i, j: (i, 0)),
            pl.BlockSpec((k, tn), lambda i, j: (0, j)),
            pl.BlockSpec((k, tn), lambda i, j: (1, j)),
            pl.BlockSpec((tm, tn), lambda i, j: (i, j)),
        ],
        out_specs=pl.BlockSpec((tm, tn), lambda i, j: (i, j)),
        out_shape=jax.ShapeDtypeStruct((m, n), F32),
        compiler_params=_cparams("parallel", "arbitrary"),
        name="outproj",
    )(o_dn, o_at, w_out, w_out, x)


CONV_LEAD = 8 + META_PAD + N_META
CONV_ROWS = 512


def _dn_prep_kernel(xq, xk, xv, mq, mk, mv, wq, wk, wv, oq, ok, ov, buf, *, seq):
    half = CONV_K // 2

    def run(x_ref, m_ref, w_ref, o_ref, normalise, scale):
        buf[0:CONV_LEAD - N_META, :] = jnp.zeros((CONV_LEAD - N_META, LANES), F32)
        buf[CONV_LEAD - N_META:CONV_LEAD, :] = m_ref[...]
        for c in range(seq // CONV_ROWS):
            buf[CONV_LEAD + c * CONV_ROWS:CONV_LEAD + (c + 1) * CONV_ROWS, :] = x_ref[c * CONV_ROWS:(c + 1) * CONV_ROWS, :]
        buf[CONV_LEAD + seq:CONV_LEAD + seq + 8, :] = jnp.zeros((8, LANES), F32)
        w = w_ref[...]

        def act(t0, n):
            y = w[0:1, :] * buf[t0 + 8 - half:t0 + 8 - half + n, :]
            for k in range(1, CONV_K):
                y = y + w[k:k + 1, :] * buf[t0 + 8 - half + k:t0 + 8 - half + k + n, :]
            y = y * jax.nn.sigmoid(y)
            if normalise:
                y = y * lax.rsqrt(jnp.sum(y * y, axis=-1, keepdims=True) + EPS) * scale
            return y.astype(o_ref.dtype)

        o_ref[0:META_PAD, :] = jnp.zeros((META_PAD, LANES), o_ref.dtype)
        o_ref[META_PAD:CHUNK, :] = act(META_PAD, N_META)
        for c in range(seq // CONV_ROWS):
            o_ref[CHUNK + c * CONV_ROWS:CHUNK + (c + 1) * CONV_ROWS, :] = act(CHUNK + c * CONV_ROWS, CONV_ROWS)

    run(xq, mq, wq, oq, True, HEAD_DIM ** -0.5)
    run(xk, mk, wk, ok, True, 1.0)
    run(xv, mv, wv, ov, False, 1.0)


def _dn_prep(proj, proj_meta, conv_w, batch, seq):
    t = seq + CHUNK
    proj3 = proj.reshape(batch, seq, BIG_DIM)
    hq, hk, hv = COL_DQ // LANES, COL_DK // LANES, COL_DV // LANES

    def xspec(off):
        return pl.BlockSpec((None, seq, LANES), lambda b, h: (b, 0, off + h))

    def mspec(off):
        return pl.BlockSpec((N_META, LANES), lambda b, h: (0, off + h))

    def wspec(off):
        return pl.BlockSpec((CONV_K, LANES), lambda b, h: (0, off + h))

    ospec = pl.BlockSpec((None, t, LANES), lambda b, h: (b, 0, h))
    oshape = jax.ShapeDtypeStruct((batch, t, DN_DIM), BF16)
    return pl.pallas_call(
        functools.partial(_dn_prep_kernel, seq=seq),
        grid=(batch, DN_HEADS),
        in_specs=[xspec(hq), xspec(hk), xspec(hv), mspec(hq), mspec(hk), mspec(hv), wspec(hq), wspec(hk), wspec(hv)],
        out_specs=[ospec, ospec, ospec],
        out_shape=[oshape, oshape, oshape],
        scratch_shapes=[pltpu.VMEM((seq + CONV_LEAD + 8, LANES), F32)],
        compiler_params=_cparams("parallel", "parallel"),
        name="dn_prep",
    )(proj3, proj3, proj3, proj_meta, proj_meta, proj_meta, conv_w, conv_w, conv_w)


def _dn_scan_kernel(qf_ref, kf_ref, vf_ref, qb_ref, kb_ref, vb_ref, abcf_ref, abrf_ref, abcb_ref, abrb_ref,
                    pc_ref, pr_ref, of_ref, ob_ref, state, o_scr):
    n = pl.program_id(1)
    n_chunks = pl.num_programs(1)

    @pl.when(n == 0)
    def _():
        state[...] = jnp.zeros_like(state)

    row = lax.broadcasted_iota(jnp.int32, (CHUNK, CHUNK), 0)
    col = lax.broadcasted_iota(jnp.int32, (CHUNK, CHUNK), 1)
    eye = (row == col).astype(F32)
    levels = []
    for sh in range(6):
        levels.append(jnp.logical_and((row >> (sh + 1)) == (col >> (sh + 1)), (row >> sh) != (col >> sh)))
    t_c = lax.broadcasted_iota(jnp.int32, (CHUNK, 1), 0)
    t_r = lax.broadcasted_iota(jnp.int32, (1, CHUNK), 1)
    hi = lax.Precision.HIGHEST

    def prepare(d, chunk, abc_ref, abr_ref):
        order = (row - col) if d == 0 else (col - row)
        incl = order >= 0
        first = chunk == 0
        live_c = jnp.logical_or(jnp.logical_not(first), t_c >= META_PAD)
        live_r = jnp.logical_or(jnp.logical_not(first), t_r >= META_PAD)
        abc = abc_ref[...]
        abr = abr_ref[...]
        beta_c = jnp.where(live_c, jax.nn.sigmoid(abc[:, 0:DN_HEADS]), 0.0)
        g_c = jnp.where(live_c, -jnp.exp(pc_ref[d, 0:1, :]) * jax.nn.softplus(abc[:, DN_HEADS:] + pc_ref[d, 1:2, :]), 0.0)
        g_r = jnp.where(live_r, -jnp.exp(pr_ref[d, :, 0:1]) * jax.nn.softplus(abr[DN_HEADS:, :] + pr_ref[d, :, 1:2]), 0.0)
        tri = incl.astype(F32)
        gcum_c = jnp.dot(tri, g_c, precision=hi, preferred_element_type=F32)
        gcum_r = lax.dot_general(g_r, tri, NT_DIMS, precision=hi, preferred_element_type=F32)
        gtot = jnp.sum(g_c, axis=0, keepdims=True)
        return dict(incl=incl, strict=order > 0, beta=beta_c, gcum_c=gcum_c, gcum_r=gcum_r,
                    eg=jnp.exp(gcum_c), ekd=jnp.exp(gtot - gcum_c), cdec=jnp.exp(gtot))

    dirs = [prepare(0, n, abcf_ref, abrf_ref), prepare(1, n_chunks - 1 - n, abcb_ref, abrb_ref)]
    qkv = [(qf_ref, kf_ref, vf_ref), (qb_ref, kb_ref, vb_ref)]

    chains = [(d, h) for d in range(2) for h in range(DN_HEADS)]
    idx = range(len(chains))
    sls = [slice(h * HEAD_DIM, (h + 1) * HEAD_DIM) for _, h in chains]
    par = [dirs[d] for d, _ in chains]
    col_of = [slice(h, h + 1) for _, h in chains]

    def dot(x, y):
        return jnp.dot(x, y, preferred_element_type=F32)

    qh = [qkv[d][0][:, sls[c]] for c, (d, _) in enumerate(chains)]
    kh = [qkv[d][1][:, sls[c]] for c, (d, _) in enumerate(chains)]
    bc = [par[c]["beta"][:, col_of[c]] for c in idx]
    egh = [par[c]["eg"][:, col_of[c]] for c in idx]
    kk = [lax.dot_general(kh[c], kh[c], NT_DIMS, preferred_element_type=F32) for c in idx]
    qk = [lax.dot_general(qh[c], kh[c], NT_DIMS, preferred_element_type=F32) for c in idx]
    decay = [jnp.exp(jnp.where(par[c]["incl"], par[c]["gcum_c"][:, col_of[c]] - par[c]["gcum_r"][col_of[c], :], -jnp.inf))
             for c in idx]
    a = [jnp.where(par[c]["strict"], bc[c] * kk[c] * decay[c], 0.0) for c in idx]
    qkd = [(qk[c] * decay[c]).astype(BF16) for c in idx]
    tinv = [eye - jnp.where(levels[0], a[c], 0.0) for c in idx]
    for lvl in levels[1:]:
        cb = [jnp.where(lvl, a[c], 0.0).astype(BF16) for c in idx]
        tb = [tinv[c].astype(BF16) for c in idx]
        tc = [dot(tb[c], cb[c]).astype(BF16) for c in idx]
        tinv = [tinv[c] - dot(tc[c], tb[c]) for c in idx]
    kf = [kh[c].astype(F32) for c in idx]
    rhs = [jnp.concatenate([qkv[d][2][:, sls[c]].astype(F32) * bc[c], kf[c] * (bc[c] * egh[c])], axis=1).astype(BF16)
           for c, (d, _) in enumerate(chains)]
    sol = [dot(tinv[c].astype(BF16), rhs[c]) for c in idx]
    q_dec = [(qh[c].astype(F32) * egh[c]).astype(BF16) for c in idx]
    k_dec = [(kf[c] * par[c]["ekd"][:, col_of[c]]).astype(BF16) for c in idx]
    s_old = [state[c] for c in idx]
    sb = [s_old[c].astype(BF16) for c in idx]
    wb = [sol[c][:, HEAD_DIM:].astype(BF16) for c in idx]
    vb = [(sol[c][:, :HEAD_DIM] - dot(wb[c], sb[c])).astype(BF16) for c in idx]
    o_inter = [dot(q_dec[c], sb[c]) for c in idx]
    for c, (d, _) in enumerate(chains):
        o_scr[d, :, sls[c]] = o_inter[c] + dot(qkd[c], vb[c])
    for c in idx:
        state[c] = s_old[c] * par[c]["cdec"][:, col_of[c]] + lax.dot_general(
            k_dec[c], vb[c], TN_DIMS, preferred_element_type=F32)

    @pl.when(n > 0)
    def _():
        of_ref[...] = o_scr[0]

    @pl.when(n < n_chunks - 1)
    def _():
        ob_ref[...] = o_scr[1]


def _dn_scan(q, k, v, ab_col, ab_row, par_col, par_row, batch, seq):
    n_chunks = (seq + CHUNK) // CHUNK
    last = n_chunks - 1
    fspec = pl.BlockSpec((None, CHUNK, DN_DIM), lambda b, n: (b, n, 0))
    bspec = pl.BlockSpec((None, CHUNK, DN_DIM), lambda b, n: (b, last - n, 0))
    oshape = jax.ShapeDtypeStruct((batch, seq, DN_DIM), F32)
    return pl.pallas_call(
        _dn_scan_kernel,
        grid=(batch, n_chunks),
        in_specs=[
            fspec, fspec, fspec, bspec, bspec, bspec,
            pl.BlockSpec((None, None, CHUNK, 2 * DN_HEADS), lambda b, n: (b, 0, n, 0)),
            pl.BlockSpec((None, None, None, 2 * DN_HEADS, CHUNK), lambda b, n: (b, 0, n, 0, 0)),
            pl.BlockSpec((None, None, CHUNK, 2 * DN_HEADS), lambda b, n: (b, 1, last - n, 0)),
            pl.BlockSpec((None, None, None, 2 * DN_HEADS, CHUNK), lambda b, n: (b, 1, last - n, 0, 0)),
            pl.BlockSpec((2, 2, DN_HEADS), lambda b, n: (0, 0, 0)),
            pl.BlockSpec((2, DN_HEADS, 2), lambda b, n: (0, 0, 0)),
        ],
        out_specs=[
            pl.BlockSpec((None, CHUNK, DN_DIM), lambda b, n: (b, jnp.maximum(n - 1, 0), 0)),
            pl.BlockSpec((None, CHUNK, DN_DIM), lambda b, n: (b, jnp.maximum(last - 1 - n, 0), 0)),
        ],
        out_shape=[oshape, oshape],
        scratch_shapes=[pltpu.VMEM((2 * DN_HEADS, HEAD_DIM, HEAD_DIM), F32), pltpu.VMEM((2, CHUNK, DN_DIM), F32)],
        compiler_params=_cparams("parallel", "arbitrary"),
        name="dn_scan",
    )(q, k, v, q, k, v, ab_col, ab_row, ab_col, ab_row, par_col, par_row)


def _dn_post_kernel(of_ref, ob_ref, g_ref, w_ref, y_ref):
    w = w_ref[...]
    for h in range(DN_HEADS):
        sl = slice(h * HEAD_DIM, (h + 1) * HEAD_DIM)
        o = of_ref[:, sl] + ob_ref[:, sl]
        o = o * lax.rsqrt(jnp.mean(o * o, axis=-1, keepdims=True) + EPS) * w
        g = g_ref[:, sl]
        y_ref[:, sl] = (o * (g * jax.nn.sigmoid(g))).astype(y_ref.dtype)


def _dn_post(o_fwd, o_bwd, proj, out_norm, batch, seq, tm):
    nb = seq // tm
    ospec = pl.BlockSpec((None, tm, DN_DIM), lambda b, i: (b, i, 0))
    return pl.pallas_call(
        _dn_post_kernel,
        grid=(batch, nb),
        in_specs=[
            ospec, ospec,
            pl.BlockSpec((tm, DN_DIM), lambda b, i: (b * nb + i, COL_GATE // DN_DIM)),
            pl.BlockSpec((1, HEAD_DIM), lambda b, i: (0, 0)),
        ],
        out_specs=pl.BlockSpec((tm, DN_DIM), lambda b, i: (b * nb + i, 0)),
        out_shape=jax.ShapeDtypeStruct((batch * seq, DN_DIM), BF16),
        compiler_params=_cparams("parallel", "parallel"),
        name="dn_post",
    )(o_fwd, o_bwd, proj, out_norm.reshape(1, HEAD_DIM))


def _rope_tables(seq):
    rows = seq // GRID_W
    row = jnp.repeat(jnp.arange(rows, dtype=F32), GRID_W)
    col = jnp.tile(jnp.arange(GRID_W, dtype=F32), rows)
    inv_freq = ROPE_THETA ** (-jnp.arange(0, AXIS_DIM, 2, dtype=F32) / AXIS_DIM)
    ar = row[:, None] * inv_freq
    ac = col[:, None] * inv_freq
    ang = jnp.concatenate([ar, ar, ac, ac], -1)
    return jnp.cos(ang), jnp.sin(ang)


def _attn_prep_kernel(q_ref, k_ref, v_ref, cos_ref, sin_ref, qw_ref, kw_ref, qo_ref, ko_ref, vo_ref):
    cos = cos_ref[...]
    sin = sin_ref[...]
    lane = lax.broadcasted_iota(jnp.int32, cos.shape, 1)
    first_half = (lane % (HEAD_DIM // 2)) < (HEAD_DIM // 4)

    def norm_rope(x, w, scale):
        x = x * lax.rsqrt(jnp.mean(x * x, axis=-1, keepdims=True) + EPS) * w
        rot = jnp.where(first_half, -pltpu.roll(x, HEAD_DIM - HEAD_DIM // 4, 1), pltpu.roll(x, HEAD_DIM // 4, 1))
        return (x * cos + rot * sin) * scale

    for h in range(AT_HEADS):
        sl = slice(h * HEAD_DIM, (h + 1) * HEAD_DIM)
        qo_ref[:, sl] = norm_rope(q_ref[:, sl], qw_ref[...], HEAD_DIM ** -0.5).astype(qo_ref.dtype)
    for h in range(AT_KV_HEADS):
        sl = slice(h * HEAD_DIM, (h + 1) * HEAD_DIM)
        ko_ref[:, sl] = norm_rope(k_ref[:, sl], kw_ref[...], 1.0).astype(ko_ref.dtype)
    vo_ref[...] = v_ref[...].astype(vo_ref.dtype)


def _attn_prep(proj, cos, sin, q_norm, k_norm, tm):
    m = proj.shape[0]
    nrep = cos.shape[0] // tm
    return pl.pallas_call(
        _attn_prep_kernel,
        grid=(m // tm,),
        in_specs=[
            pl.BlockSpec((tm, AT_Q_DIM), lambda i: (i, COL_AQ // AT_Q_DIM)),
            pl.BlockSpec((tm, AT_KV_DIM), lambda i: (i, COL_AK // AT_KV_DIM)),
            pl.BlockSpec((tm, AT_KV_DIM), lambda i: (i, COL_AV // AT_KV_DIM)),
            pl.BlockSpec((tm, HEAD_DIM), lambda i: (i % nrep, 0)),
            pl.BlockSpec((tm, HEAD_DIM), lambda i: (i % nrep, 0)),
            pl.BlockSpec((1, HEAD_DIM), lambda i: (0, 0)),
            pl.BlockSpec((1, HEAD_DIM), lambda i: (0, 0)),
        ],
        out_specs=[
            pl.BlockSpec((tm, AT_Q_DIM), lambda i: (i, 0)),
            pl.BlockSpec((tm, AT_KV_DIM), lambda i: (i, 0)),
            pl.BlockSpec((tm, AT_KV_DIM), lambda i: (i, 0)),
        ],
        out_shape=[
            jax.ShapeDtypeStruct((m, AT_Q_DIM), BF16),
            jax.ShapeDtypeStruct((m, AT_KV_DIM), BF16),
            jax.ShapeDtypeStruct((m, AT_KV_DIM), BF16),
        ],
        compiler_params=_cparams("parallel"),
        name="attn_prep",
    )(proj, proj, proj, cos, sin, q_norm.reshape(1, HEAD_DIM), k_norm.reshape(1, HEAD_DIM))


def _flash_kernel(q_ref, k_ref, v_ref, km_ref, vm_ref, w_ref, o_ref, o_scr, *, seq, tq, tk):
    lane = lax.broadcasted_iota(jnp.int32, (1, LANES), 1)
    is_meta = lane < N_META
    n_kv = seq // tk
    for g in range(AT_KV_HEADS):
        ks = slice(g * HEAD_DIM, (g + 1) * HEAD_DIM)
        qg = jnp.concatenate(
            [q_ref[:, (AT_GROUP * g + j) * HEAD_DIM:(AT_GROUP * g + j + 1) * HEAD_DIM] for j in range(AT_GROUP)], axis=0)
        s0 = lax.dot_general(qg, km_ref[:, ks], NT_DIMS, preferred_element_type=F32)
        s0 = jnp.where(is_meta, s0, -jnp.inf)
        m0 = jnp.max(s0, axis=-1, keepdims=True)
        p0 = jnp.exp(s0 - m0)
        l0 = jnp.sum(p0, axis=-1, keepdims=True)
        acc0 = jnp.dot(p0.astype(BF16), vm_ref[:, ks], preferred_element_type=F32)

        def body(j, carry):
            m, l, acc = carry
            start = pl.multiple_of(j * tk, tk)
            kb = k_ref[pl.ds(start, tk), ks]
            vb = v_ref[pl.ds(start, tk), ks]
            s = lax.dot_general(qg, kb, NT_DIMS, preferred_element_type=F32)
            m_new = jnp.maximum(m, jnp.max(s, axis=-1, keepdims=True))
            alpha = jnp.exp(m - m_new)
            p = jnp.exp(s - m_new)
            l = alpha * l + jnp.sum(p, axis=-1, keepdims=True)
            acc = alpha * acc + jnp.dot(p.astype(BF16), vb, preferred_element_type=F32)
            return m_new, l, acc

        unroll = max(u for u in (1, 2, 4) if n_kv % u == 0)
        _, l, acc = lax.fori_loop(0, n_kv, body, (m0, l0, acc0), unroll=unroll)
        og = acc / l
        for j in range(AT_GROUP):
            h = AT_GROUP * g + j
            o_scr[:, h * HEAD_DIM:(h + 1) * HEAD_DIM] = og[j * tq:(j + 1) * tq, :]
    o = o_scr[...]
    o_ref[...] = (o * lax.rsqrt(jnp.mean(o * o, axis=-1, keepdims=True) + EPS) * w_ref[...]).astype(o_ref.dtype)


def _flash(q, k, v, k_meta, v_meta, out_norm, batch, seq, tq, tk):
    nq = seq // tq
    return pl.pallas_call(
        functools.partial(_flash_kernel, seq=seq, tq=tq, tk=tk),
        grid=(batch, nq),
        in_specs=[
            pl.BlockSpec((tq, AT_Q_DIM), lambda b, i: (b * nq + i, 0)),
            pl.BlockSpec((seq, AT_KV_DIM), lambda b, i: (b, 0)),
            pl.BlockSpec((seq, AT_KV_DIM), lambda b, i: (b, 0)),
            pl.BlockSpec((LANES, AT_KV_DIM), lambda b, i: (0, 0)),
            pl.BlockSpec((LANES, AT_KV_DIM), lambda b, i: (0, 0)),
            pl.BlockSpec((1, AT_Q_DIM), lambda b, i: (0, 0)),
        ],
        out_specs=pl.BlockSpec((tq, AT_Q_DIM), lambda b, i: (b * nq + i, 0)),
        out_shape=jax.ShapeDtypeStruct((batch * seq, AT_Q_DIM), BF16),
        scratch_shapes=[pltpu.VMEM((tq, AT_Q_DIM), F32)],
        compiler_params=_cparams("parallel", "arbitrary"),
        name="flash",
    )(q, k, v, k_meta, v_meta, out_norm.reshape(1, AT_Q_DIM))


def _sort_network(n):
    pairs = []

    def merge(lo, cnt, r):
        step = 2 * r
        if step < cnt:
            merge(lo, cnt, step)
            merge(lo + r, cnt, step)
            pairs.extend((i, i + r) for i in range(lo + r, lo + cnt - r, step))
        else:
            pairs.append((lo, lo + r))

    def sort(lo, cnt):
        if cnt > 1:
            sort(lo, cnt // 2)
            sort(lo + cnt // 2, cnt // 2)
            merge(lo, cnt, 1)

    sort(0, n)
    return pairs


SUBLANES = 8


def _top_desc_tiles(x, count):
    n = x.shape[0] // SUBLANES
    tiles = [x[SUBLANES * v:SUBLANES * (v + 1), :] for v in range(n)]
    for i, j in _sort_network(n):
        tiles[i], tiles[j] = jnp.maximum(tiles[i], tiles[j]), jnp.minimum(tiles[i], tiles[j])
    sub = lax.broadcasted_iota(jnp.int32, tiles[0].shape, 0).astype(F32)
    out = []
    for k in range(count):
        head = tiles[0]
        m = jnp.max(head, axis=0, keepdims=True)
        out.append(m)
        if k == count - 1:
            break
        first = jnp.min(jnp.where(head == m, sub, float(SUBLANES)), axis=0, keepdims=True)
        pop = sub == first
        for v in range(min(n, count - k - 1)):
            below = tiles[v + 1] if v + 1 < n else jnp.full_like(head, -jnp.inf)
            tiles[v] = jnp.where(pop, below, tiles[v])
    return out


def _peer_route_kernel(h_ref, wq_ref, keys_ref, q_ref, s2_ref, a_ref, b_ref):
    q_t = lax.dot_general(wq_ref[...], h_ref[...], NT_DIMS, preferred_element_type=F32)
    half = PEER_KEY_DIM // 2
    s1 = jnp.dot(keys_ref[0], q_t[:half].astype(BF16), preferred_element_type=F32)
    s2 = jnp.dot(keys_ref[1], q_t[half:].astype(BF16), preferred_element_type=F32)
    t1 = _top_desc_tiles(s1, PEER_TOPK + 1)
    t2 = _top_desc_tiles(s2, PEER_TOPK + 1)
    t2cat = jnp.concatenate(t2[:PEER_TOPK], axis=0)
    t1cat = jnp.concatenate(t1[:PEER_TOPK], axis=0)
    oct_ = PEER_TOPK // 2
    fill = jnp.full_like(t1[0], -jnp.inf)
    edge = jnp.concatenate([t1[PEER_TOPK] + t2[0], t1[0] + t2[PEER_TOPK]] + [fill] * 6, axis=0)
    groups = [t1[0] + t2cat] + [t1[i] + t2cat[:oct_] for i in range(1, oct_)] + [t1cat[oct_:] + t2[0], edge]
    rows = sum(g.shape[0] for g in groups)
    pad_rows = pl.next_power_of_2(rows // SUBLANES) * SUBLANES - rows
    cand = jnp.concatenate(groups + [jnp.full((pad_rows, fill.shape[1]), -jnp.inf, F32)], axis=0)
    top = _top_desc_tiles(cand, PEER_TOPK + 1)
    z = jnp.exp(top[0] - top[0])
    for i in range(1, PEER_TOPK):
        z = z + jnp.exp(top[i] - top[0])
    thr = 0.5 * (top[PEER_TOPK - 1] + top[PEER_TOPK])
    q_ref[...] = thr - s1
    s2_ref[...] = s2
    a_ref[...] = jnp.exp(s1 - t1[0]) * (0.5 / z)
    b_ref[...] = jnp.exp(s2 - t2[0])


def _peer_route(hn, wq_t, keys, tm):
    m, d = hn.shape
    spec = pl.BlockSpec((None, N_KEYS, tm), lambda i, h: (h, 0, i))
    shape = jax.ShapeDtypeStruct((PEER_HEADS, N_KEYS, m), F32)
    return pl.pallas_call(
        _peer_route_kernel,
        grid=(m // tm, PEER_HEADS),
        in_specs=[
            pl.BlockSpec((tm, d), lambda i, h: (i, 0)),
            pl.BlockSpec((PEER_KEY_DIM, d), lambda i, h: (h, 0)),
            pl.BlockSpec((None, 2, N_KEYS, PEER_KEY_DIM // 2), lambda i, h: (h, 0, 0, 0)),
        ],
        out_specs=[spec, spec, spec, spec],
        out_shape=[shape, shape, shape, shape],
        compiler_params=_cparams("parallel", "arbitrary"),
        name="peer_route",
    )(hn, wq_t, keys)


def _peer_act_kernel(hn_ref, u_ref, q_ref, s2_ref, a_ref, b_ref, act_ref, *, te):
    j = pl.program_id(1)
    s_t = lax.dot_general(u_ref[...], hn_ref[...], NT_DIMS, preferred_element_type=F32)
    per = te // N_KEYS
    for ii in range(per):
        i = j * per + ii
        gate = None
        for h in range(PEER_HEADS):
            term = jnp.where(s2_ref[h] >= q_ref[h, pl.ds(i, 1), :], b_ref[h], 0.0) * a_ref[h, pl.ds(i, 1), :]
            gate = term if gate is None else gate + term
        x = s_t[ii * N_KEYS:(ii + 1) * N_KEYS, :]
        act = x * (1.0 + lax.erf(x * (2.0 ** -0.5))) * gate
        act_ref[ii * N_KEYS:(ii + 1) * N_KEYS, :] = act.astype(act_ref.dtype)


def _peer_act(hn, u, q, s2, a, b, tm, te):
    m, d = hn.shape
    n_exp = u.shape[0]
    once = pl.Buffered(1)
    tok = pl.BlockSpec((PEER_HEADS, N_KEYS, tm), lambda i, j: (0, 0, i), pipeline_mode=once)
    return pl.pallas_call(
        functools.partial(_peer_act_kernel, te=te),
        grid=(m // tm, n_exp // te),
        in_specs=[
            pl.BlockSpec((tm, d), lambda i, j: (i, 0), pipeline_mode=once),
            pl.BlockSpec((te, d), lambda i, j: (j, 0)),
            tok, tok, tok, tok,
        ],
        out_specs=pl.BlockSpec((te, tm), lambda i, j: (j, i)),
        out_shape=jax.ShapeDtypeStruct((n_exp, m), BF16),
        compiler_params=_cparams("parallel", "arbitrary"),
        name="peer_act",
    )(hn, u, q, s2, a, b)


def _peer_out_kernel(act_ref, v_ref, res_ref, o_ref):
    k = pl.program_id(1)

    @pl.when(k == 0)
    def _():
        o_ref[...] = res_ref[...]

    o_ref[...] += lax.dot_general(act_ref[...], v_ref[...], TN_DIMS, preferred_element_type=F32)


def _peer_out(act_t, v, res, tm, tk):
    n_exp, m = act_t.shape
    d = v.shape[1]
    once = pl.Buffered(1)
    return pl.pallas_call(
        _peer_out_kernel,
        grid=(m // tm, n_exp // tk),
        in_specs=[
            pl.BlockSpec((tk, tm), lambda i, k: (k, i)),
            pl.BlockSpec((tk, d), lambda i, k: (k, 0)),
            pl.BlockSpec((tm, d), lambda i, k: (i, 0), pipeline_mode=once),
        ],
        out_specs=pl.BlockSpec((tm, d), lambda i, k: (i, 0), pipeline_mode=once),
        out_shape=jax.ShapeDtypeStruct((m, d), F32),
        compiler_params=_cparams("parallel", "arbitrary"),
        name="peer_out",
    )(act_t, v, res)


def _tile(m, pref):
    return pref if m % pref == 0 else m


def _encode(x, meta, wts):
    batch, seq, d = x.shape
    m = batch * seq
    xf = x.reshape(m, d)

    xn = _rmsnorm(xf, wts["norm1"], _tile(m, 256))
    proj = _matmul(xn, wts["w_big"], _tile(m, 1024), 1024)
    ab = _matmul(xn, wts["w_small"], _tile(m, 1024), LANES)
    proj_meta, ab_meta = meta["proj"], meta["ab"]

    q, k, v = _dn_prep(proj, proj_meta, wts["conv_w"], batch, seq)
    t = seq + CHUNK
    ab_seq = jnp.concatenate([
        jnp.zeros((batch, META_PAD, 4 * DN_HEADS), F32),
        jnp.broadcast_to(ab_meta[None, :, :4 * DN_HEADS], (batch, N_META, 4 * DN_HEADS)),
        ab[:, :4 * DN_HEADS].reshape(batch, seq, 4 * DN_HEADS)], axis=1)
    ab_seq = ab_seq.reshape(batch, t, 2, 2, DN_HEADS)
    ab_col = jnp.transpose(ab_seq, (0, 3, 1, 2, 4)).reshape(batch, 2, t, 2 * DN_HEADS)
    ab_row = jnp.swapaxes(ab_col.reshape(batch, 2, t // CHUNK, CHUNK, 2 * DN_HEADS), -1, -2)
    o_fwd, o_bwd = _dn_scan(q, k, v, ab_col, ab_row, wts["dn_par_col"], wts["dn_par_row"], batch, seq)
    o_dn = _dn_post(o_fwd, o_bwd, proj, wts["dn_out_norm"], batch, seq, _tile(seq, 256))

    cos, sin = _rope_tables(seq)
    tp = _tile(seq, 256)
    aq, ak, av = _attn_prep(proj, cos, sin, wts["q_norm"], wts["k_norm"], tp)
    o_at = _flash(aq, ak, av, meta["k"], meta["v"], wts["attn_out_norm"], batch, seq, _tile(seq, 256), _tile(seq, 1024))

    h2 = _outproj(o_dn, o_at, wts["w_out"], xf, _tile(m, 1024), 1024)
    h2n = _rmsnorm(h2, wts["norm2"], _tile(m, 256))
    q, s2, a, b = _peer_route(h2n, wts["peer_wq_t"], wts["peer_keys"], _tile(m, 512))
    act_t = _peer_act(h2n, wts["peer_u"], q, s2, a, b, _tile(m, 512), 1024)
    y = _peer_out(act_t, wts["peer_v"], h2, _tile(m, 512), 1024)
    return y.reshape(batch, seq, d)


def kernel(x_prompt, x_sample, meta_tokens, norm1, w_in, conv_w, a_log, dt_bias, dn_out_norm, q_norm, k_norm,
           attn_out_norm, w_out, norm2, peer_wq, peer_keys, peer_u, peer_v):
    w = w_in[0]
    offs = [0]
    for n in IN_SPLITS:
        offs.append(offs[-1] + n)
    part = [w[:, offs[i]:offs[i + 1]] for i in range(len(IN_SPLITS))]
    w_big = jnp.concatenate([part[0], part[1], part[2], part[5], part[6], part[7], part[8]], axis=1).astype(BF16)
    w_small = jnp.concatenate(
        [part[3], part[4], jnp.zeros((w.shape[0], LANES - 4 * DN_HEADS), w.dtype)], axis=1).astype(BF16)
    cw = conv_w[0]
    wts = {
        "norm1": norm1[0],
        "w_big": w_big,
        "w_small": w_small,
        "conv_w": cw,
        "dn_par_col": jnp.stack([a_log[0], dt_bias[0]], axis=1),
        "dn_par_row": jnp.stack([a_log[0], dt_bias[0]], axis=2),
        "dn_out_norm": dn_out_norm[0],
        "q_norm": q_norm[0],
        "k_norm": k_norm[0],
        "attn_out_norm": attn_out_norm[0],
        "w_out": w_out[0].astype(BF16),
        "norm2": norm2[0],
        "peer_wq_t": peer_wq[0].T.astype(BF16),
        "peer_keys": peer_keys[0].astype(BF16),
        "peer_u": peer_u[0].astype(BF16),
        "peer_v": peer_v[0].astype(BF16),
    }

    mn = _rmsnorm(meta_tokens, wts["norm1"], N_META)
    proj_meta = _matmul(mn, w_big, N_META, 512)
    ab_meta = _matmul(mn, w_small, N_META, LANES)
    ones = jnp.ones((N_META, HEAD_DIM), F32)
    _, k_meta, v_meta = _attn_prep(proj_meta, ones, jnp.zeros_like(ones), wts["q_norm"], wts["k_norm"], N_META)
    pad = ((0, LANES - N_META), (0, 0))
    meta = {"proj": proj_meta, "ab": ab_meta, "k": jnp.pad(k_meta, pad), "v": jnp.pad(v_meta, pad)}

    return (_encode(x_prompt, meta, wts), _encode(x_sample, meta, wts))
```
